```python
import functools
import math
import jax, jax.numpy as jnp
from jax import lax
import numpy as np

D_MODEL = 4096
BATCH = 4
SEQ = 2048
DEPTH = 1
DEC_BATCH = 32
DEC_SEQ = 8
PAST_LEN = 8192
PAGE_SIZE = 128

HEAD_DIM = 128
N_HEADS_A = D_MODEL // 256
ATTN_W = N_HEADS_A * HEAD_DIM
SSM_W = D_MODEL // 2
SSM_CH = 16
N_GROUPS = SSM_W // SSM_CH
SSM_P = 64
D_FF = ((8 * D_MODEL // 3 + 255) // 256) * 256
PLE_DIM = 256
Q_BLOCK = 128
EPS = 1e-6
FORGET_BIAS = 4.0
DT_MIN = 0.001
DT_MAX = 0.1
N_IN = 3 * ATTN_W + N_HEADS_A + SSM_W + 2 * D_MODEL

kernel_name = 'fox_s5_macaron_hybrid_step'


def _rms_norm(x, g):
    xf = x.astype(jnp.float32)
    y = xf * lax.rsqrt(jnp.mean(xf * xf, axis=-1, keepdims=True) + EPS)
    return (y * g.astype(jnp.float32)).astype(x.dtype)


def _swiglu(h, w_gate, w_up, w_down):
    return (jax.nn.silu(h @ w_gate) * (h @ w_up)) @ w_down


def _project(h, w_in, b_f, q_norm, k_norm):
    z = h @ w_in
    lead = z.shape[:-1]
    o3 = 3 * ATTN_W
    o4 = o3 + N_HEADS_A
    o5 = o4 + SSM_W
    o6 = o5 + D_MODEL
    q, k, v, f, u, g_a, g_b = jnp.split(z, [ATTN_W, 2 * ATTN_W, o3, o4, o5, o6], axis=-1)
    hd = lead + (N_HEADS_A, HEAD_DIM)
    q = _rms_norm(q.reshape(hd), q_norm)
    k = _rms_norm(k.reshape(hd), k_norm)
    v = v.reshape(hd)
    logf = jax.nn.log_sigmoid((f + b_f).astype(jnp.float32))
    return q, k, v, logf, u, g_a, g_b


def _attn_prompt(q, k, v, logf):
    bsz, s_len = q.shape[0], q.shape[1]
    scale = HEAD_DIM ** -0.5
    c = jnp.cumsum(logf, axis=1).transpose(0, 2, 1)
    pos_k = jnp.arange(s_len)

    def block(i):
        q0 = i * Q_BLOCK
        qb = lax.dynamic_slice_in_dim(q, q0, Q_BLOCK, axis=1)
        cq = lax.dynamic_slice_in_dim(c, q0, Q_BLOCK, axis=2)
        s = jnp.einsum('bthd,bshd->bhts', qb, k, preferred_element_type=jnp.float32) * scale
        s = s + cq[..., :, None] - c[..., None, :]
        pos_q = q0 + jnp.arange(Q_BLOCK)
        s = jnp.where(pos_k[None, :] <= pos_q[:, None], s, -jnp.inf)
        pr = jax.nn.softmax(s, axis=-1).astype(v.dtype)
        return jnp.einsum('bhts,bshd->bthd', pr, v)

    out = lax.map(block, jnp.arange(s_len // Q_BLOCK))
    return out.transpose(1, 0, 2, 3, 4).reshape(bsz, s_len, ATTN_W)


def _attn_sample(q, k, v, logf, cache_k, cache_v, cache_logf, page_table):
    n_seq, t_len = q.shape[0], q.shape[1]
    scale = HEAD_DIM ** -0.5

    def one(args):
        qb, kb, vb, lfb, pages = args
        k_all = jnp.concatenate([cache_k[pages].reshape(-1, N_HEADS_A, HEAD_DIM).astype(kb.dtype), kb], axis=0)
        v_all = jnp.concatenate([cache_v[pages].reshape(-1, N_HEADS_A, HEAD_DIM).astype(vb.dtype), vb], axis=0)
        lf_all = jnp.concatenate([cache_logf[pages].reshape(-1, N_HEADS_A).astype(jnp.float32), lfb], axis=0)
        past = lf_all.shape[0] - t_len
        c = jnp.cumsum(lf_all, axis=0).T
        s = jnp.einsum('thd,shd->hts', qb, k_all, preferred_element_type=jnp.float32) * scale
        s = s + c[:, past:, None] - c[:, None, :]
        pos_k = jnp.arange(past + t_len)
        pos_q = past + jnp.arange(t_len)
        s = jnp.where(pos_k[None, :] <= pos_q[:, None], s, -jnp.inf)
        pr = jax.nn.softmax(s, axis=-1).astype(v_all.dtype)
        return jnp.einsum('hts,shd->thd', pr, v_all)

    out = lax.map(one, (q, k, v, logf, page_table))
    return out.reshape(n_seq, t_len, ATTN_W)


def _ssm_combine(e1, e2):
    a1r, a1i, b1r, b1i = e1
    a2r, a2i, b2r, b2i = e2
    return (a1r * a2r - a1i * a2i, a1r * a2i + a1i * a2r,
            a2r * b1r - a2i * b1i + b2r, a2r * b1i + a2i * b1r + b2i)


def _s5_branch(u, a_re, a_im, log_dt, b_re, b_im, c_re, c_im, d, w_glu, init_re, init_im):
    f32 = jnp.float32
    bsz, t_len = u.shape[0], u.shape[1]
    uf = u.astype(f32).reshape(bsz, t_len, N_GROUPS, SSM_CH)
    ar = a_re.astype(f32)
    ai = a_im.astype(f32)
    dt = jnp.exp(log_dt.astype(f32))[:, None]
    mag = jnp.exp(dt * ar)
    ang = dt * ai
    abar_re = mag * jnp.cos(ang)
    abar_im = mag * jnp.sin(ang)
    den = ar * ar + ai * ai
    nr = abar_re - 1.0
    coef_re = (nr * ar + abar_im * ai) / den
    coef_im = (abar_im * ar - nr * ai) / den
    br = b_re.astype(f32)
    bi = b_im.astype(f32)
    bb_re = coef_re[..., None] * br - coef_im[..., None] * bi
    bb_im = coef_re[..., None] * bi + coef_im[..., None] * br
    bu_re = jnp.einsum('btgc,gpc->btgp', uf, bb_re)
    bu_im = jnp.einsum('btgc,gpc->btgp', uf, bb_im)
    a_seq_re = jnp.broadcast_to(abar_re, (1, t_len, N_GROUPS, SSM_P))
    a_seq_im = jnp.broadcast_to(abar_im, (1, t_len, N_GROUPS, SSM_P))
    _, _, xr, xi = lax.associative_scan(_ssm_combine, (a_seq_re, a_seq_im, bu_re, bu_im), axis=1)
    if init_re is not None:
        tt = jnp.arange(1, t_len + 1, dtype=f32)[:, None, None]
        pmag = jnp.exp(tt * dt * ar)
        pang = tt * dt * ai
        pw_re = pmag * jnp.cos(pang)
        pw_im = pmag * jnp.sin(pang)
        s0r = init_re.astype(f32)[:, None]
        s0i = init_im.astype(f32)[:, None]
        xr = xr + pw_re * s0r - pw_im * s0i
        xi = xi + pw_re * s0i + pw_im * s0r
    y = (jnp.einsum('gcp,btgp->btgc', c_re.astype(f32), xr)
         - jnp.einsum('gcp,btgp->btgc', c_im.astype(f32), xi)
         + d.astype(f32) * uf).reshape(bsz, t_len, SSM_W)
    yg = jax.nn.gelu(y)
    out = yg * jax.nn.sigmoid(yg @ w_glu.astype(f32))
    return out.astype(u.dtype), xr[:, -1], xi[:, -1]


def _layer(x, p, attend, init_re, init_im, lw):
    x = x + 0.5 * _swiglu(_rms_norm(x, lw['ffn1_norm']), lw['ffn1_w_gate'], lw['ffn1_w_up'], lw['ffn1_w_down'])
    h = _rms_norm(x, lw['mix_norm'])
    q, k, v, logf, u, g_a, g_b = _project(h, lw['w_in'], lw['b_f'], lw['q_norm'], lw['k_norm'])
    a = attend(q, k, v, logf)
    b, s_re, s_im = _s5_branch(u, lw['ssm_a_re'], lw['ssm_a_im'], lw['ssm_log_dt'], lw['ssm_b_re'],
                               lw['ssm_b_im'], lw['ssm_c_re'], lw['ssm_c_im'], lw['ssm_d'],
                               lw['ssm_w_glu'], init_re, init_im)
    merged = jax.nn.sigmoid(g_a) * (a @ lw['w_branch_a']) + jax.nn.sigmoid(g_b) * (b @ lw['w_branch_b'])
    x = x + merged @ lw['w_out']
    x = x + 0.5 * _swiglu(_rms_norm(x, lw['ffn2_norm']), lw['ffn2_w_gate'], lw['ffn2_w_up'], lw['ffn2_w_down'])
    x = x + (p @ lw['w_ple']) * jax.nn.sigmoid(_rms_norm(x, lw['ple_norm']) @ lw['w_ple_gate'])
    return x, k, v, logf, s_re, s_im


def setup_inputs(seed: int = 0) -> dict:
    key = jax.random.key(seed)
    ks = jax.random.split(key, 40)
    f32 = jnp.float32
    n_pages = PAST_LEN // PAGE_SIZE
    n_phys = (DEC_BATCH * n_pages * 5) // 4

    def nrm(k, shape, scale=1.0):
        return jax.random.normal(k, shape, f32) * scale

    def gain(k, shape):
        return 1.0 + 0.02 * jax.random.normal(k, shape, f32)

    page_table = jax.random.permutation(ks[0], n_phys)[:DEC_BATCH * n_pages].reshape(DEC_BATCH, n_pages).astype(jnp.int32)
    x_prompt = nrm(ks[1], (BATCH, SEQ, D_MODEL))
    x_sample = nrm(ks[2], (DEC_BATCH, DEC_SEQ, D_MODEL))
    cache_k = nrm(ks[3], (DEPTH, n_phys, PAGE_SIZE, N_HEADS_A, HEAD_DIM))
    cache_v = nrm(ks[4], (DEPTH, n_phys, PAGE_SIZE, N_HEADS_A, HEAD_DIM))
    cache_logf = jax.nn.log_sigmoid(FORGET_BIAS + nrm(ks[5], (DEPTH, n_phys, PAGE_SIZE, N_HEADS_A)))
    state_ssm_re = nrm(ks[6], (DEPTH, DEC_BATCH, N_GROUPS, SSM_P), 0.5)
    state_ssm_im = nrm(ks[7], (DEPTH, DEC_BATCH, N_GROUPS, SSM_P), 0.5)
    p_prompt = nrm(ks[8], (DEPTH, BATCH, SEQ, PLE_DIM))
    p_sample = nrm(ks[9], (DEPTH, DEC_BATCH, DEC_SEQ, PLE_DIM))
    ffn1_norm = gain(ks[10], (DEPTH, D_MODEL))
    ffn1_w_gate = nrm(ks[11], (DEPTH, D_MODEL, D_FF), D_MODEL ** -0.5)
    ffn1_w_up = nrm(ks[12], (DEPTH, D_MODEL, D_FF), D_MODEL ** -0.5)
    ffn1_w_down = nrm(ks[13], (DEPTH, D_FF, D_MODEL), D_FF ** -0.5)
    mix_norm = gain(ks[14], (DEPTH, D_MODEL))
    w_in = nrm(ks[15], (DEPTH, D_MODEL, N_IN), D_MODEL ** -0.5)
    b_f = FORGET_BIAS + nrm(ks[16], (DEPTH, N_HEADS_A), 0.1)
    q_norm = gain(ks[17], (DEPTH, HEAD_DIM))
    k_norm = gain(ks[18], (DEPTH, HEAD_DIM))
    n_idx = jnp.arange(SSM_P, dtype=f32)
    ssm_a_re = -0.5 + nrm(ks[19], (DEPTH, N_GROUPS, SSM_P), 0.01)
    ssm_a_im = math.pi * n_idx + nrm(ks[20], (DEPTH, N_GROUPS, SSM_P), 0.01)
    ssm_log_dt = jax.random.uniform(ks[21], (DEPTH, N_GROUPS), f32, math.log(DT_MIN), math.log(DT_MAX))
    ssm_b_re = nrm(ks[22], (DEPTH, N_GROUPS, SSM_P, SSM_CH), (2 * SSM_CH) ** -0.5)
    ssm_b_im = nrm(ks[23], (DEPTH, N_GROUPS, SSM_P, SSM_CH), (2 * SSM_CH) ** -0.5)
    ssm_c_re = nrm(ks[24], (DEPTH, N_GROUPS, SSM_CH, SSM_P), SSM_P ** -0.5)
    ssm_c_im = nrm(ks[25], (DEPTH, N_GROUPS, SSM_CH, SSM_P), SSM_P ** -0.5)
    ssm_d = nrm(ks[26], (DEPTH, N_GROUPS, SSM_CH))
    ssm_w_glu = nrm(ks[27], (DEPTH, SSM_W, SSM_W), SSM_W ** -0.5)
    w_branch_a = nrm(ks[28], (DEPTH, ATTN_W, D_MODEL), ATTN_W ** -0.5)
    w_branch_b = nrm(ks[29], (DEPTH, SSM_W, D_MODEL), SSM_W ** -0.5)
    w_out = nrm(ks[30], (DEPTH, D_MODEL, D_MODEL), D_MODEL ** -0.5)
    ffn2_norm = gain(ks[31], (DEPTH, D_MODEL))
    ffn2_w_gate = nrm(ks[32], (DEPTH, D_MODEL, D_FF), D_MODEL ** -0.5)
    ffn2_w_up = nrm(ks[33], (DEPTH, D_MODEL, D_FF), D_MODEL ** -0.5)
    ffn2_w_down = nrm(ks[34], (DEPTH, D_FF, D_MODEL), D_FF ** -0.5)
    ple_norm = gain(ks[35], (DEPTH, D_MODEL))
    w_ple = nrm(ks[36], (DEPTH, PLE_DIM, D_MODEL), PLE_DIM ** -0.5)
    w_ple_gate = nrm(ks[37], (DEPTH, D_MODEL, D_MODEL), D_MODEL ** -0.5)
    return {
        'x_prompt': x_prompt, 'x_sample': x_sample,
        'cache_k': cache_k, 'cache_v': cache_v, 'cache_logf': cache_logf,
        'state_ssm_re': state_ssm_re, 'state_ssm_im': state_ssm_im,
        'page_table': page_table, 'p_prompt': p_prompt, 'p_sample': p_sample,
        'ffn1_norm': ffn1_norm, 'ffn1_w_gate': ffn1_w_gate, 'ffn1_w_up': ffn1_w_up, 'ffn1_w_down': ffn1_w_down,
        'mix_norm': mix_norm, 'w_in': w_in, 'b_f': b_f, 'q_norm': q_norm, 'k_norm': k_norm,
        'ssm_a_re': ssm_a_re, 'ssm_a_im': ssm_a_im, 'ssm_log_dt': ssm_log_dt,
        'ssm_b_re': ssm_b_re, 'ssm_b_im': ssm_b_im, 'ssm_c_re': ssm_c_re, 'ssm_c_im': ssm_c_im,
        'ssm_d': ssm_d, 'ssm_w_glu': ssm_w_glu,
        'w_branch_a': w_branch_a, 'w_branch_b': w_branch_b, 'w_out': w_out,
        'ffn2_norm': ffn2_norm, 'ffn2_w_gate': ffn2_w_gate, 'ffn2_w_up': ffn2_w_up, 'ffn2_w_down': ffn2_w_down,
        'ple_norm': ple_norm, 'w_ple': w_ple, 'w_ple_gate': w_ple_gate,
    }


def reference(x_prompt, x_sample, cache_k, cache_v, cache_logf, state_ssm_re, state_ssm_im,
              page_table, p_prompt, p_sample,
              ffn1_norm, ffn1_w_gate, ffn1_w_up, ffn1_w_down,
              mix_norm, w_in, b_f, q_norm, k_norm,
              ssm_a_re, ssm_a_im, ssm_log_dt, ssm_b_re, ssm_b_im, ssm_c_re, ssm_c_im, ssm_d, ssm_w_glu,
              w_branch_a, w_branch_b, w_out,
              ffn2_norm, ffn2_w_gate, ffn2_w_up, ffn2_w_down,
              ple_norm, w_ple, w_ple_gate):
    xp = x_prompt
    xs = x_sample
    kp_l, vp_l, lfp_l, srp_l, sip_l = [], [], [], [], []
    ks_l, vs_l, lfs_l, srs_l, sis_l = [], [], [], [], []
    for i in range(DEPTH):
        lw = {
            'ffn1_norm': ffn1_norm[i], 'ffn1_w_gate': ffn1_w_gate[i], 'ffn1_w_up': ffn1_w_up[i], 'ffn1_w_down': ffn1_w_down[i],
            'mix_norm': mix_norm[i], 'w_in': w_in[i], 'b_f': b_f[i], 'q_norm': q_norm[i], 'k_norm': k_norm[i],
            'ssm_a_re': ssm_a_re[i], 'ssm_a_im': ssm_a_im[i], 'ssm_log_dt': ssm_log_dt[i],
            'ssm_b_re': ssm_b_re[i], 'ssm_b_im': ssm_b_im[i], 'ssm_c_re': ssm_c_re[i], 'ssm_c_im': ssm_c_im[i],
            'ssm_d': ssm_d[i], 'ssm_w_glu': ssm_w_glu[i],
            'w_branch_a': w_branch_a[i], 'w_branch_b': w_branch_b[i], 'w_out': w_out[i],
            'ffn2_norm': ffn2_norm[i], 'ffn2_w_gate': ffn2_w_gate[i], 'ffn2_w_up': ffn2_w_up[i], 'ffn2_w_down': ffn2_w_down[i],
            'ple_norm': ple_norm[i], 'w_ple': w_ple[i], 'w_ple_gate': w_ple_gate[i],
        }
        xp, kp, vp, lfp, srp, sip = _layer(xp, p_prompt[i], _attn_prompt, None, None, lw)
        attend_s = functools.partial(_attn_sample, cache_k=cache_k[i], cache_v=cache_v[i],
                                     cache_logf=cache_logf[i], page_table=page_table)
        xs, kss, vss, lfs, srs, sis = _layer(xs, p_sample[i], attend_s, state_ssm_re[i], state_ssm_im[i], lw)
        kp_l.append(kp); vp_l.append(vp); lfp_l.append(lfp); srp_l.append(srp); sip_l.append(sip)
        ks_l.append(kss); vs_l.append(vss); lfs_l.append(lfs); srs_l.append(srs); sis_l.append(sis)
    return (xp, xs,
            jnp.stack(kp_l), jnp.stack(vp_l), jnp.stack(lfp_l), jnp.stack(srp_l), jnp.stack(sip_l),
            jnp.stack(ks_l), jnp.stack(vs_l), jnp.stack(lfs_l), jnp.stack(srs_l), jnp.stack(sis_l))
```

```python
import functools
import math

import jax
import jax.numpy as jnp
from jax import lax
from jax.experimental import pallas as pl
from jax.experimental.pallas import tpu as pltpu

F32 = jnp.float32
BF16 = jnp.bfloat16

EPS = 1e-6
HEAD_DIM = 128
SSM_CH = 16
LANES = 128
VMEM_LIMIT_BYTES = 56 * 1024 * 1024
NEG_BIG = -1e30


def _params(sem):
    return pltpu.CompilerParams(dimension_semantics=sem, vmem_limit_bytes=VMEM_LIMIT_BYTES)


def _tile(n, target, mult):
    best = None
    for t in range(mult, min(n, target) + 1, mult):
        if n % t == 0:
            best = t
    assert best is not None, (n, target, mult)
    return best


def _rmsnorm_kernel(x_ref, g_ref, o_ref):
    x = x_ref[...]
    y = x * lax.rsqrt(jnp.mean(x * x, axis=-1, keepdims=True) + EPS)
    o_ref[...] = (y * g_ref[...]).astype(o_ref.dtype)


def _rmsnorm(x, gain):
    m, d = x.shape
    tm = _tile(m, 512, 16)
    return pl.pallas_call(
        _rmsnorm_kernel,
        grid=(m // tm,),
        in_specs=[pl.BlockSpec((tm, d), lambda i: (i, 0)),
                  pl.BlockSpec((1, d), lambda i: (0, 0))],
        out_specs=pl.BlockSpec((tm, d), lambda i: (i, 0)),
        out_shape=jax.ShapeDtypeStruct((m, d), BF16),
        compiler_params=_params(("parallel",)),
    )(x, gain.reshape(1, d))


def _ffn_kernel(x_hbm, g_ref, wg_ref, wu_ref, wd_ref, o_hbm, acc, h, sem, *, tm, rc):
    i = pl.program_id(0)
    k = pl.program_id(1)

    @pl.when(k == 0)
    def _():
        cp = pltpu.make_async_copy(x_hbm.at[pl.ds(i * tm, tm)], acc, sem.at[0])
        cp.start()
        cp.wait()

        def norm_rows(r, carry):
            rows = pl.ds(pl.multiple_of(r * rc, rc), rc)
            xs = acc[rows, :]
            y = xs * lax.rsqrt(jnp.mean(xs * xs, axis=-1, keepdims=True) + EPS)
            h[rows, :] = (y * g_ref[...]).astype(BF16)
            return carry

        lax.fori_loop(0, tm // rc, norm_rows, 0)

    hb = h[...]
    g = jnp.dot(hb, wg_ref[...].astype(BF16), preferred_element_type=F32)
    u = jnp.dot(hb, wu_ref[...].astype(BF16), preferred_element_type=F32)
    a = (0.5 * (g * jax.nn.sigmoid(g)) * u).astype(BF16)
    acc[...] += jnp.dot(a, wd_ref[...].astype(BF16), preferred_element_type=F32)

    @pl.when(k == pl.num_programs(1) - 1)
    def _():
        cp = pltpu.make_async_copy(acc, o_hbm.at[pl.ds(i * tm, tm)], sem.at[1])
        cp.start()
        cp.wait()


def _ffn(x, gain, w_gate, w_up, w_down):
    m, d = x.shape
    dff = w_gate.shape[1]
    tm = _tile(m, 768, 16)
    tf = _tile(dff, 256, LANES)
    rc = _tile(tm, 64, 8)
    return pl.pallas_call(
        functools.partial(_ffn_kernel, tm=tm, rc=rc),
        grid=(m // tm, dff // tf),
        in_specs=[pl.BlockSpec(memory_space=pl.ANY),
                  pl.BlockSpec((1, d), lambda i, k: (0, 0)),
                  pl.BlockSpec((d, tf), lambda i, k: (0, k)),
                  pl.BlockSpec((d, tf), lambda i, k: (0, k)),
                  pl.BlockSpec((tf, d), lambda i, k: (k, 0))],
        out_specs=pl.BlockSpec(memory_space=pl.ANY),
        out_shape=jax.ShapeDtypeStruct((m, d), F32),
        scratch_shapes=[pltpu.VMEM((tm, d), F32),
                        pltpu.VMEM((tm, d), BF16),
                        pltpu.SemaphoreType.DMA((2,))],
        compiler_params=_params(("arbitrary", "arbitrary")),
    )(x, gain.reshape(1, d), w_gate, w_up, w_down)


def _mm_kernel(*refs, n_lhs, dots, n_rows, n_vecs, epilogue):
    lhs_refs = refs[:n_lhs]
    w_refs = refs[n_lhs:n_lhs + len(dots)]
    p = n_lhs + len(dots)
    row_refs = refs[p:p + n_rows]
    vec_refs = refs[p + n_rows:p + n_rows + n_vecs]
    out_refs = refs[p + n_rows + n_vecs:]
    lhs_vals = [r[...] for r in lhs_refs]
    accs = [jnp.dot(lhs_vals[li], w_ref[...].astype(BF16), preferred_element_type=F32)
            for li, w_ref in zip(dots, w_refs)]
    outs = epilogue(accs, [r[...] for r in row_refs], [v[...] for v in vec_refs])
    for o_ref, o in zip(out_refs, outs):
        o_ref[...] = o.astype(o_ref.dtype)


def _mm(lhs, ws, epilogue, out_dtypes, n, *, rows=(), vecs=(), tm_target=1056, tn_target=512):
    m = lhs[0].shape[0]
    tm = _tile(m, tm_target, 16)
    tn = _tile(n, tn_target, LANES)
    for _, _, off in ws:
        assert off % tn == 0
    for _, off in rows:
        assert off % tn == 0
    in_specs = [pl.BlockSpec((tm, a.shape[1]), lambda i, j: (i, 0)) for a in lhs]
    in_specs += [pl.BlockSpec((w.shape[0], tn), functools.partial(lambda i, j, o: (0, j + o), o=off // tn))
                 for w, _, off in ws]
    in_specs += [pl.BlockSpec((tm, tn), functools.partial(lambda i, j, o: (i, j + o), o=off // tn))
                 for _, off in rows]
    in_specs += [pl.BlockSpec((1, tn), lambda i, j: (0, j)) for _ in vecs]
    out_specs = [pl.BlockSpec((tm, tn), lambda i, j: (i, j)) for _ in out_dtypes]
    out_shape = [jax.ShapeDtypeStruct((m, n), dt) for dt in out_dtypes]
    kern = functools.partial(_mm_kernel, n_lhs=len(lhs), dots=tuple(li for _, li, _ in ws),
                             n_rows=len(rows), n_vecs=len(vecs), epilogue=epilogue)
    outs = pl.pallas_call(
        kern,
        grid=(m // tm, n // tn),
        in_specs=in_specs,
        out_specs=out_specs,
        out_shape=out_shape,
        compiler_params=_params(("parallel", "arbitrary")),
    )(*lhs, *[w for w, _, _ in ws], *[r for r, _ in rows], *vecs)
    return outs


def _head_norm_epilogue(accs, rows, vecs):
    z = accs[0]
    gain = vecs[0]
    parts = []
    for c in range(z.shape[1] // HEAD_DIM):
        zc = z[:, c * HEAD_DIM:(c + 1) * HEAD_DIM]
        y = zc * lax.rsqrt(jnp.mean(zc * zc, axis=-1, keepdims=True) + EPS)
        parts.append(y * gain[:, c * HEAD_DIM:(c + 1) * HEAD_DIM])
    return [jnp.concatenate(parts, axis=1) if len(parts) > 1 else parts[0]]


def _log_sigmoid(x):
    return jnp.minimum(x, 0.0) - jnp.log1p(jnp.exp(-jnp.abs(x)))


def _cumsum_kernel(x_ref, o_ref):
    x = x_ref[0]
    s = x.shape[0]
    row = lax.broadcasted_iota(jnp.int32, x.shape, 0)
    k = 1
    while k < s:
        x = x + jnp.where(row >= k, pltpu.roll(x, k, 0), 0.0)
        k *= 2
    o_ref[0] = x


def _cumsum_seq(x):
    b, s, w = x.shape
    return pl.pallas_call(
        _cumsum_kernel,
        grid=(b,),
        in_specs=[pl.BlockSpec((1, s, w), lambda i: (i, 0, 0))],
        out_specs=pl.BlockSpec((1, s, w), lambda i: (i, 0, 0)),
        out_shape=jax.ShapeDtypeStruct((b, s, w), F32),
        compiler_params=_params(("parallel",)),
    )(x)


def _attn_prompt_kernel(q_ref, k_ref, v_ref, c_ref, ct_ref, o_ref, *, tq, scale):
    h = pl.program_id(1)
    qi = pl.program_id(2)
    q = q_ref[0]
    c_blk = c_ref[0]
    lane = lax.broadcasted_iota(jnp.int32, c_blk.shape, 1)
    c_col = jnp.sum(jnp.where(lane == h, c_blk, 0.0), axis=1, keepdims=True)
    q_pos = qi * tq + lax.broadcasted_iota(jnp.int32, (tq, tq), 0)
    k_off = lax.broadcasted_iota(jnp.int32, (tq, tq), 1)

    def step(j, carry):
        m, l, acc = carry
        start = pl.multiple_of(j * tq, tq)
        kb = k_ref[0, pl.ds(start, tq), :].astype(BF16)
        vb = v_ref[0, pl.ds(start, tq), :].astype(BF16)
        c_row = ct_ref[0, pl.ds(h, 1), pl.ds(start, tq)]
        s = lax.dot_general(q, kb, (((1,), (1,)), ((), ())), preferred_element_type=F32) * scale
        s = s + c_col - c_row
        s = jnp.where(j * tq + k_off <= q_pos, s, NEG_BIG)
        m_new = jnp.maximum(m, jnp.max(s, axis=1, keepdims=True))
        p = jnp.exp(s - m_new)
        alpha = jnp.exp(m - m_new)
        l = alpha * l + jnp.sum(p, axis=1, keepdims=True)
        acc = alpha * acc + jnp.dot(p.astype(BF16), vb, preferred_element_type=F32)
        return m_new, l, acc

    init = (jnp.full((tq, 1), NEG_BIG, F32), jnp.zeros((tq, 1), F32), jnp.zeros((tq, HEAD_DIM), F32))
    _, l, acc = lax.fori_loop(0, qi + 1, step, init)
    o_ref[0] = (acc / l).astype(o_ref.dtype)


def _attn_prompt(q, k, v, c, c_t, n_heads):
    b, s, _ = q.shape
    tq = _tile(s, 256, LANES)
    return pl.pallas_call(
        functools.partial(_attn_prompt_kernel, tq=tq, scale=HEAD_DIM ** -0.5),
        grid=(b, n_heads, s // tq),
        in_specs=[pl.BlockSpec((1, tq, HEAD_DIM), lambda bi, h, qi: (bi, qi, h)),
                  pl.BlockSpec((1, s, HEAD_DIM), lambda bi, h, qi: (bi, 0, h)),
                  pl.BlockSpec((1, s, HEAD_DIM), lambda bi, h, qi: (bi, 0, h)),
                  pl.BlockSpec((1, tq, LANES), lambda bi, h, qi: (bi, qi, 0)),
                  pl.BlockSpec((1, n_heads, s), lambda bi, h, qi: (bi, 0, 0))],
        out_specs=pl.BlockSpec((1, tq, HEAD_DIM), lambda bi, h, qi: (bi, qi, h)),
        out_shape=jax.ShapeDtypeStruct(q.shape, BF16),
        compiler_params=_params(("parallel", "parallel", "arbitrary")),
    )(q, k, v, c, c_t)


def _attn_sample_kernel(pt_ref, q_ref, kn_ref, vn_ref, lfn_ref, lfnt_ref, *rest,
                        n_heads, pages_per_step, page, t_len, scale):
    npg = pages_per_step
    k_refs = rest[:npg]
    v_refs = rest[npg:2 * npg]
    lf_refs = rest[2 * npg:3 * npg]
    o_ref = rest[3 * npg]
    m_ref, l_ref, acc_ref, d_ref, carry_ref, cn_ref = rest[3 * npg + 1:]
    g = pl.program_id(1)
    lane = lax.broadcasted_iota(jnp.int32, (t_len, LANES), 1)
    lane_h = lax.broadcasted_iota(jnp.int32, (n_heads, LANES), 1)

    def online_update(hh, s, vb):
        m_old = m_ref[hh]
        m_new = jnp.maximum(m_old, jnp.max(s, axis=1, keepdims=True))
        p = jnp.exp(s - m_new)
        alpha = jnp.exp(m_old - m_new)
        l_ref[hh] = alpha * l_ref[hh] + jnp.sum(p, axis=1, keepdims=True)
        acc_ref[hh] = alpha * acc_ref[hh] + jnp.dot(p.astype(BF16), vb, preferred_element_type=F32)
        m_ref[hh] = m_new

    @pl.when(g == 0)
    def _():
        sub = lax.broadcasted_iota(jnp.int32, (t_len, LANES), 0)
        cn = lfn_ref[0]
        k = 1
        while k < t_len:
            cn = cn + jnp.where(sub >= k, pltpu.roll(cn, k, 0), 0.0)
            k *= 2
        cn_ref[...] = cn
        cnt = lfnt_ref[0]
        k = 1
        while k < t_len:
            cnt = cnt + jnp.where(lane_h >= k, pltpu.roll(cnt, k, 1), 0.0)
            k *= 2
        carry_ref[...] = jnp.zeros_like(carry_ref)
        m_ref[...] = jnp.full_like(m_ref, NEG_BIG)
        l_ref[...] = jnp.zeros_like(l_ref)
        acc_ref[...] = jnp.zeros_like(acc_ref)
        d_ref[0] = cnt
        t_pos = lax.broadcasted_iota(jnp.int32, (t_len, page), 0)
        s_pos = lax.broadcasted_iota(jnp.int32, (t_len, page), 1)

        def new_head(hh, carry):
            col = pl.ds(pl.multiple_of(hh * HEAD_DIM, HEAD_DIM), HEAD_DIM)
            qh = q_ref[0, :, col]
            kh = kn_ref[0, :, col].astype(BF16)
            vh = vn_ref[0, :, col].astype(BF16)
            c_col = jnp.sum(jnp.where(lane == hh, cn, 0.0), axis=1, keepdims=True)
            c_row = d_ref[0, pl.ds(hh, 1), :]
            s = lax.dot_general(qh, kh, (((1,), (1,)), ((), ())), preferred_element_type=F32) * scale
            s = jnp.where(s_pos <= t_pos, s + c_col - c_row, NEG_BIG)
            online_update(hh, s, vh)
            return carry

        lax.fori_loop(0, n_heads, new_head, 0)

    for pg in range(npg):
        lf = lf_refs[pg][0]
        sfx = lf
        k = 1
        while k < page:
            sfx = sfx + jnp.where(lane_h + k < page, pltpu.roll(sfx, page - k, 1), 0.0)
            k *= 2
        d_ref[pg] = sfx - lf + carry_ref[...]
        carry_ref[...] = carry_ref[...] + jnp.sum(lf, axis=1, keepdims=True)

    cn = cn_ref[...]

    def head(hh, carry):
        col = pl.ds(pl.multiple_of(hh * HEAD_DIM, HEAD_DIM), HEAD_DIM)
        qh = q_ref[0, :, col]
        c_col = jnp.sum(jnp.where(lane == hh, cn, 0.0), axis=1, keepdims=True)
        for pg in range(npg):
            kh = k_refs[pg][0, pl.ds(hh, page, stride=n_heads), :].astype(BF16)
            vh = v_refs[pg][0, pl.ds(hh, page, stride=n_heads), :].astype(BF16)
            s = lax.dot_general(qh, kh, (((1,), (1,)), ((), ())), preferred_element_type=F32) * scale
            s = s + c_col + d_ref[pg, pl.ds(hh, 1), :]
            online_update(hh, s, vh)
        return carry

    lax.fori_loop(0, n_heads, head, 0)

    @pl.when(g == pl.num_programs(1) - 1)
    def _():
        for hh in range(n_heads):
            o_ref[0, :, hh * HEAD_DIM:(hh + 1) * HEAD_DIM] = (acc_ref[hh] / l_ref[hh]).astype(o_ref.dtype)


def _attn_sample(q, k_new, v_new, lf_new, lf_new_t, cache_k, cache_v, cache_lf_t, page_table, n_heads):
    db, t_len, w = q.shape
    n_pages = page_table.shape[1]
    page = cache_lf_t.shape[2]
    npg = _tile(n_pages, 4, 1)

    def page_map(b, g, pt, *, pg):
        return (pt[b, n_pages - 1 - (g * npg + pg)], 0, 0)

    def seq_map(b, g, pt):
        return (b, 0, 0)

    in_specs = [pl.BlockSpec((1, t_len, w), seq_map),
                pl.BlockSpec((1, page, w), seq_map),
                pl.BlockSpec((1, page, w), seq_map),
                pl.BlockSpec((1, t_len, LANES), seq_map),
                pl.BlockSpec((1, n_heads, LANES), seq_map)]
    in_specs += [pl.BlockSpec((1, page * n_heads, HEAD_DIM), functools.partial(page_map, pg=pg)) for pg in range(npg)]
    in_specs += [pl.BlockSpec((1, page * n_heads, HEAD_DIM), functools.partial(page_map, pg=pg)) for pg in range(npg)]
    in_specs += [pl.BlockSpec((1, n_heads, page), functools.partial(page_map, pg=pg)) for pg in range(npg)]
    kern = functools.partial(_attn_sample_kernel, n_heads=n_heads, pages_per_step=npg, page=page,
                             t_len=t_len, scale=HEAD_DIM ** -0.5)
    return pl.pallas_call(
        kern,
        grid_spec=pltpu.PrefetchScalarGridSpec(
            num_scalar_prefetch=1,
            grid=(db, n_pages // npg),
            in_specs=in_specs,
            out_specs=pl.BlockSpec((1, t_len, w), seq_map),
            scratch_shapes=[pltpu.VMEM((n_heads, t_len, 1), F32),
                            pltpu.VMEM((n_heads, t_len, 1), F32),
                            pltpu.VMEM((n_heads, t_len, HEAD_DIM), F32),
                            pltpu.VMEM((npg, n_heads, page), F32),
                            pltpu.VMEM((n_heads, 1), F32),
                            pltpu.VMEM((t_len, LANES), F32)]),
        out_shape=jax.ShapeDtypeStruct(q.shape, BF16),
        compiler_params=_params(("parallel", "arbitrary")),
    )(page_table, q, k_new, v_new, lf_new, lf_new_t,
      *([cache_k] * npg), *([cache_v] * npg), *([cache_lf_t] * npg))


def _s5_kernel(u_ref, bb_ref, cc_ref, a_ref, d_ref, init_ref, yg_ref, ygb_ref, fin_ref, x_scr, st_scr,
               *, nb, tc, half, exact):
    tci = pl.program_id(1)

    @pl.when(tci == 0)
    def _():
        st_scr[...] = init_ref[0]

    u = u_ref[...]
    if exact:
        x_scr[...] = jnp.dot(u, bb_ref[0], precision=lax.Precision.HIGHEST, preferred_element_type=F32)
    else:
        x_scr[...] = jnp.dot(u.astype(BF16), bb_ref[0].astype(BF16), preferred_element_type=F32)
    a_re = a_ref[0, :, :half]
    a_im = a_ref[0, :, half:]

    def step(t, carry):
        xr, xi = carry
        rows = pl.ds(pl.multiple_of(t * nb, nb), nb)
        bu = x_scr[rows, :]
        nr = a_re * xr - a_im * xi + bu[:, :half]
        ni = a_re * xi + a_im * xr + bu[:, half:]
        x_scr[rows, :] = jnp.concatenate([nr, ni], axis=1)
        return nr, ni

    st = st_scr[...]
    xr, xi = lax.fori_loop(0, tc, step, (st[:, :half], st[:, half:]))
    st_scr[...] = jnp.concatenate([xr, xi], axis=1)
    x = x_scr[...]
    if exact:
        y = jnp.dot(x, cc_ref[0], precision=lax.Precision.HIGHEST, preferred_element_type=F32)
    else:
        y = jnp.dot(x.astype(BF16), cc_ref[0].astype(BF16), preferred_element_type=F32)
    y = y + d_ref[...] * u
    yg = jax.nn.gelu(y, approximate=True)
    yg_ref[...] = yg
    ygb_ref[...] = yg.astype(BF16)

    @pl.when(tci == pl.num_programs(1) - 1)
    def _():
        fin_ref[0] = st_scr[...]


def _s5(u_tb, bb, cc, abar, d, init, *, nb, exact):
    rows, w = u_tb.shape
    t_len = rows // nb
    nj, _, two_half = bb.shape
    half = two_half // 2
    tc = _tile(t_len, max(1, 1024 // nb), 1)
    kern = functools.partial(_s5_kernel, nb=nb, tc=tc, half=half, exact=exact)
    return pl.pallas_call(
        kern,
        grid=(nj, t_len // tc),
        in_specs=[pl.BlockSpec((tc * nb, LANES), lambda j, t: (t, j)),
                  pl.BlockSpec((1, LANES, two_half), lambda j, t: (j, 0, 0)),
                  pl.BlockSpec((1, two_half, LANES), lambda j, t: (j, 0, 0)),
                  pl.BlockSpec((1, 1, two_half), lambda j, t: (j, 0, 0)),
                  pl.BlockSpec((1, LANES), lambda j, t: (0, j)),
                  pl.BlockSpec((1, nb, two_half), lambda j, t: (j, 0, 0))],
        out_specs=[pl.BlockSpec((tc * nb, LANES), lambda j, t: (t, j)),
                   pl.BlockSpec((tc * nb, LANES), lambda j, t: (t, j)),
                   pl.BlockSpec((1, nb, two_half), lambda j, t: (j, 0, 0))],
        out_shape=[jax.ShapeDtypeStruct((rows, w), F32),
                   jax.ShapeDtypeStruct((rows, w), BF16),
                   jax.ShapeDtypeStruct((nj, nb, two_half), F32)],
        scratch_shapes=[pltpu.VMEM((tc * nb, two_half), F32),
                        pltpu.VMEM((nb, two_half), F32)],
        compiler_params=_params(("parallel", "arbitrary")),
    )(u_tb, bb, cc, abar, d.reshape(1, w), init)


def _s5_params(a_re, a_im, log_dt, b_re, b_im, c_re, c_im):
    g, p = a_re.shape
    ch = b_re.shape[2]
    gpb = LANES // ch
    nj = g // gpb
    dt = jnp.exp(log_dt)[:, None]
    mag = jnp.exp(dt * a_re)
    ang = dt * a_im
    abar_re = mag * jnp.cos(ang)
    abar_im = mag * jnp.sin(ang)
    den = a_re * a_re + a_im * a_im
    nr = abar_re - 1.0
    coef_re = (nr * a_re + abar_im * a_im) / den
    coef_im = (abar_im * a_re - nr * a_im) / den
    bb_re = coef_re[..., None] * b_re - coef_im[..., None] * b_im
    bb_im = coef_re[..., None] * b_im + coef_im[..., None] * b_re
    eye = jnp.eye(gpb, dtype=F32)

    def pack_in(m):
        m = m.reshape(nj, gpb, p, ch).transpose(0, 1, 3, 2)
        return jnp.einsum('jgcp,gh->jgchp', m, eye).reshape(nj, gpb * ch, gpb * p)

    def pack_out(m):
        m = m.reshape(nj, gpb, ch, p).transpose(0, 1, 3, 2)
        return jnp.einsum('jgpc,gh->jgphc', m, eye).reshape(nj, gpb * p, gpb * ch)

    bb = jnp.concatenate([pack_in(bb_re), pack_in(bb_im)], axis=2)
    cc = jnp.concatenate([pack_out(c_re), pack_out(-c_im)], axis=1)
    abar = jnp.concatenate([abar_re.reshape(nj, 1, gpb * p), abar_im.reshape(nj, 1, gpb * p)], axis=2)
    return bb, cc, abar


def _pack_state(s_re, s_im, nj):
    nb = s_re.shape[0]
    return jnp.concatenate([s_re.reshape(nb, nj, -1).transpose(1, 0, 2),
                            s_im.reshape(nb, nj, -1).transpose(1, 0, 2)], axis=2)


def _unpack_state(fin, n_real, g, p):
    nj, nb, two_half = fin.shape
    half = two_half // 2
    re = fin[:, :n_real, :half].transpose(1, 0, 2).reshape(n_real, g, p)
    im = fin[:, :n_real, half:].transpose(1, 0, 2).reshape(n_real, g, p)
    return re, im


def kernel(x_prompt, x_sample, cache_k, cache_v, cache_logf, state_ssm_re, state_ssm_im, page_table, p_prompt, p_sample, ffn1_norm, ffn1_w_gate, ffn1_w_up, ffn1_w_down, mix_norm, w_in, b_f, q_norm, k_norm, ssm_a_re, ssm_a_im, ssm_log_dt, ssm_b_re, ssm_b_im, ssm_c_re, ssm_c_im, ssm_d, ssm_w_glu, w_branch_a, w_branch_b, w_out, ffn2_norm, ffn2_w_gate, ffn2_w_up, ffn2_w_down, ple_norm, w_ple, w_ple_gate):
    depth = ffn1_norm.shape[0]
    assert depth == 1
    bsz, seq, dm = x_prompt.shape
    dbsz, dseq, _ = x_sample.shape
    n_heads = cache_k.shape[3]
    attn_w = n_heads * HEAD_DIM
    n_groups, ssm_p = ssm_a_re.shape[1], ssm_a_re.shape[2]
    ssm_w = n_groups * SSM_CH
    n_phys, page = cache_k.shape[1], cache_k.shape[2]
    mp = bsz * seq
    ms = dbsz * dseq
    m = mp + ms
    o3 = 3 * attn_w
    o4 = o3 + n_heads

    x = jnp.concatenate([x_prompt.reshape(mp, dm), x_sample.reshape(ms, dm)], axis=0)
    pe_in = jnp.concatenate([p_prompt[0].reshape(mp, -1), p_sample[0].reshape(ms, -1)], axis=0).astype(BF16)

    x1 = _ffn(x, ffn1_norm[0], ffn1_w_gate[0], ffn1_w_up[0], ffn1_w_down[0])

    h2 = _rmsnorm(x1, mix_norm[0])
    w_in0 = w_in[0]
    w_rest = w_in0[:, o4:]
    w_f = jnp.pad(w_in0[:, o3:o4], ((0, 0), (0, LANES - n_heads)))
    b_f_pad = jnp.pad(b_f[0], (0, LANES - n_heads)).reshape(1, LANES)
    qg = jnp.tile(q_norm[0], n_heads).reshape(1, attn_w)
    kg = jnp.tile(k_norm[0], n_heads).reshape(1, attn_w)
    (q,) = _mm([h2], [(w_in0, 0, 0)], _head_norm_epilogue, [BF16], attn_w, vecs=[qg])
    (k,) = _mm([h2], [(w_in0, 0, attn_w)], _head_norm_epilogue, [F32], attn_w, vecs=[kg])
    (v,) = _mm([h2], [(w_in0, 0, 2 * attn_w)], lambda a, r, c: [a[0]], [F32], attn_w)
    (logf,) = _mm([h2], [(w_f, 0, 0)], lambda a, r, c: [_log_sigmoid(a[0] + c[0])], [F32], LANES, vecs=[b_f_pad])
    (u,) = _mm([h2], [(w_rest, 0, 0)], lambda a, r, c: [a[0]], [F32], ssm_w)
    gate_a, gate_b = _mm([h2], [(w_rest, 0, ssm_w), (w_rest, 0, ssm_w + dm)],
                         lambda a, r, c: [jax.nn.sigmoid(a[0]), jax.nn.sigmoid(a[1])], [F32, F32], dm,
                         tn_target=256)

    lf_p = logf[:mp].reshape(bsz, seq, LANES)
    c_p = _cumsum_seq(lf_p)
    c_p_t = c_p[:, :, :n_heads].transpose(0, 2, 1)
    a_p = _attn_prompt(q[:mp].reshape(bsz, seq, attn_w), k[:mp].reshape(bsz, seq, attn_w),
                       v[:mp].reshape(bsz, seq, attn_w), c_p, c_p_t, n_heads)

    lf_s = logf[mp:].reshape(dbsz, dseq, LANES)
    lf_s_t = jnp.pad(lf_s[:, :, :n_heads].transpose(0, 2, 1), ((0, 0), (0, 0), (0, LANES - dseq)))
    pad_new = ((0, 0), (0, page - dseq), (0, 0))
    k_new = jnp.pad(k[mp:].reshape(dbsz, dseq, attn_w), pad_new)
    v_new = jnp.pad(v[mp:].reshape(dbsz, dseq, attn_w), pad_new)
    a_s = _attn_sample(q[mp:].reshape(dbsz, dseq, attn_w), k_new, v_new, lf_s, lf_s_t,
                       cache_k[0].reshape(n_phys, page * n_heads, HEAD_DIM),
                       cache_v[0].reshape(n_phys, page * n_heads, HEAD_DIM),
                       cache_logf[0].transpose(0, 2, 1), page_table, n_heads)
    a = jnp.concatenate([a_p.reshape(mp, attn_w), a_s.reshape(ms, attn_w)], axis=0)

    bb, cc, abar = _s5_params(ssm_a_re[0], ssm_a_im[0], ssm_log_dt[0], ssm_b_re[0], ssm_b_im[0],
                              ssm_c_re[0], ssm_c_im[0])
    nj = bb.shape[0]
    d_vec = ssm_d[0].reshape(ssm_w)
    nb_p = -(-bsz // 8) * 8
    u_p = jnp.pad(u[:mp].reshape(bsz, seq, ssm_w).transpose(1, 0, 2), ((0, 0), (0, nb_p - bsz), (0, 0)))
    init_p = jnp.zeros((nj, nb_p, bb.shape[2]), F32)
    yg_p, ygb_p, fin_p = _s5(u_p.reshape(seq * nb_p, ssm_w), bb, cc, abar, d_vec, init_p, nb=nb_p, exact=False)
    nb_s = -(-dbsz // 8) * 8
    u_s = jnp.pad(u[mp:].reshape(dbsz, dseq, ssm_w).transpose(1, 0, 2), ((0, 0), (0, nb_s - dbsz), (0, 0)))
    pad_state = ((0, nb_s - dbsz), (0, 0), (0, 0))
    init_s = _pack_state(jnp.pad(state_ssm_re[0], pad_state), jnp.pad(state_ssm_im[0], pad_state), nj)
    yg_s, ygb_s, fin_s = _s5(u_s.reshape(dseq * nb_s, ssm_w), bb, cc, abar, d_vec, init_s, nb=nb_s, exact=True)

    def untb(y, t_len, nb, n_real):
        return y.reshape(t_len, nb, ssm_w)[:, :n_real].transpose(1, 0, 2).reshape(n_real * t_len, ssm_w)

    yg = jnp.concatenate([untb(yg_p, seq, nb_p, bsz), untb(yg_s, dseq, nb_s, dbsz)], axis=0)
    ygb = jnp.concatenate([untb(ygb_p, seq, nb_p, bsz), untb(ygb_s, dseq, nb_s, dbsz)], axis=0)
    (b_out,) = _mm([ygb], [(ssm_w_glu[0], 0, 0)], lambda a, r, c: [r[0] * jax.nn.sigmoid(a[0])], [BF16], ssm_w,
                   rows=[(yg, 0)])

    (merged,) = _mm([a, b_out], [(w_branch_a[0], 0, 0), (w_branch_b[0], 1, 0)],
                    lambda acc, r, c: [r[0] * acc[0] + r[1] * acc[1]], [BF16], dm,
                    rows=[(gate_a, 0), (gate_b, 0)])
    (x2,) = _mm([merged], [(w_out[0], 0, 0)], lambda acc, r, c: [r[0] + acc[0]], [F32], dm, rows=[(x1, 0)])

    x3 = _ffn(x2, ffn2_norm[0], ffn2_w_gate[0], ffn2_w_up[0], ffn2_w_down[0])

    h4 = _rmsnorm(x3, ple_norm[0])
    (x4,) = _mm([pe_in, h4], [(w_ple[0], 0, 0), (w_ple_gate[0], 1, 0)],
                lambda acc, r, c: [r[0] + acc[0] * jax.nn.sigmoid(acc[1])], [F32], dm, rows=[(x3, 0)])

    sre_p, sim_p = _unpack_state(fin_p, bsz, n_groups, ssm_p)
    sre_s, sim_s = _unpack_state(fin_s, dbsz, n_groups, ssm_p)
    return (x4[:mp].reshape(bsz, seq, dm), x4[mp:].reshape(dbsz, dseq, dm),
            k[:mp].reshape(1, bsz, seq, n_heads, HEAD_DIM), v[:mp].reshape(1, bsz, seq, n_heads, HEAD_DIM),
            lf_p[:, :, :n_heads].reshape(1, bsz, seq, n_heads),
            sre_p[None], sim_p[None],
            k[mp:].reshape(1, dbsz, dseq, n_heads, HEAD_DIM), v[mp:].reshape(1, dbsz, dseq, n_heads, HEAD_DIM),
            lf_s[:, :, :n_heads].reshape(1, dbsz, dseq, n_heads),
            sre_s[None], sim_s[None])
```

```python
import functools
import math

import jax
import jax.numpy as jnp
from jax import lax
from jax.experimental import pallas as pl
from jax.experimental.pallas import tpu as pltpu

F32 = jnp.float32
BF16 = jnp.bfloat16

EPS = 1e-6
HEAD_DIM = 128
SSM_CH = 16
LANES = 128
VMEM_LIMIT_BYTES = 56 * 1024 * 1024
NEG_BIG = -1e30


def _params(sem):
    return pltpu.CompilerParams(dimension_semantics=sem, vmem_limit_bytes=VMEM_LIMIT_BYTES)


def _tile(n, target, mult):
    best = None
    for t in range(mult, min(n, target) + 1, mult):
        if n % t == 0:
            best = t
    assert best is not None, (n, target, mult)
    return best


def _rmsnorm_kernel(x_ref, g_ref, o_ref):
    x = x_ref[...]
    y = x * lax.rsqrt(jnp.mean(x * x, axis=-1, keepdims=True) + EPS)
    o_ref[...] = (y * g_ref[...]).astype(o_ref.dtype)


def _rmsnorm(x, gain):
    m, d = x.shape
    tm = _tile(m, 512, 16)
    return pl.pallas_call(
        _rmsnorm_kernel,
        grid=(m // tm,),
        in_specs=[pl.BlockSpec((tm, d), lambda i: (i, 0)),
                  pl.BlockSpec((1, d), lambda i: (0, 0))],
        out_specs=pl.BlockSpec((tm, d), lambda i: (i, 0)),
        out_shape=jax.ShapeDtypeStruct((m, d), BF16),
        compiler_params=_params(("parallel",)),
    )(x, gain.reshape(1, d))


def _ffn_kernel(x_hbm, g_ref, wg_ref, wu_ref, wd_ref, o_hbm, acc, h, sem, *, tm, rc):
    i = pl.program_id(0)
    k = pl.program_id(1)

    @pl.when(k == 0)
    def _():
        cp = pltpu.make_async_copy(x_hbm.at[pl.ds(i * tm, tm)], acc, sem.at[0])
        cp.start()
        cp.wait()

        def norm_rows(r, carry):
            rows = pl.ds(pl.multiple_of(r * rc, rc), rc)
            xs = acc[rows, :]
            y = xs * lax.rsqrt(jnp.mean(xs * xs, axis=-1, keepdims=True) + EPS)
            h[rows, :] = (y * g_ref[...]).astype(BF16)
            return carry

        lax.fori_loop(0, tm // rc, norm_rows, 0)

    hb = h[...]
    g = jnp.dot(hb, wg_ref[...].astype(BF16), preferred_element_type=F32)
    u = jnp.dot(hb, wu_ref[...].astype(BF16), preferred_element_type=F32)
    a = (0.5 * (g * jax.nn.sigmoid(g)) * u).astype(BF16)
    acc[...] += jnp.dot(a, wd_ref[...].astype(BF16), preferred_element_type=F32)

    @pl.when(k == pl.num_programs(1) - 1)
    def _():
        cp = pltpu.make_async_copy(acc, o_hbm.at[pl.ds(i * tm, tm)], sem.at[1])
        cp.start()
        cp.wait()


def _ffn(x, gain, w_gate, w_up, w_down):
    m, d = x.shape
    dff = w_gate.shape[1]
    tm = _tile(m, 768, 16)
    tf = _tile(dff, 256, LANES)
    rc = _tile(tm, 64, 8)
    return pl.pallas_call(
        functools.partial(_ffn_kernel, tm=tm, rc=rc),
        grid=(m // tm, dff // tf),
        in_specs=[pl.BlockSpec(memory_space=pl.ANY),
                  pl.BlockSpec((1, d), lambda i, k: (0, 0)),
                  pl.BlockSpec((d, tf), lambda i, k: (0, k)),
                  pl.BlockSpec((d, tf), lambda i, k: (0, k)),
                  pl.BlockSpec((tf, d), lambda i, k: (k, 0))],
        out_specs=pl.BlockSpec(memory_space=pl.ANY),
        out_shape=jax.ShapeDtypeStruct((m, d), F32),
        scratch_shapes=[pltpu.VMEM((tm, d), F32),
                        pltpu.VMEM((tm, d), BF16),
                        pltpu.SemaphoreType.DMA((2,))],
        compiler_params=_params(("arbitrary", "arbitrary")),
    )(x, gain.reshape(1, d), w_gate, w_up, w_down)


def _mxu_dot(lhs, w):
    if lhs.dtype == BF16 and w.dtype != BF16:
        w = w.astype(BF16)
    return jnp.dot(lhs, w, preferred_element_type=F32)


def _mm_kernel(*refs, n_lhs, dots, n_rows, n_vecs, epilogue):
    lhs_refs = refs[:n_lhs]
    w_refs = refs[n_lhs:n_lhs + len(dots)]
    p = n_lhs + len(dots)
    row_refs = refs[p:p + n_rows]
    vec_refs = refs[p + n_rows:p + n_rows + n_vecs]
    out_refs = refs[p + n_rows + n_vecs:]
    lhs_vals = [r[...] for r in lhs_refs]
    accs = [_mxu_dot(lhs_vals[li], w_ref[...]) for li, w_ref in zip(dots, w_refs)]
    outs = epilogue(accs, [r[...] for r in row_refs], [v[...] for v in vec_refs])
    for o_ref, o in zip(out_refs, outs):
        o_ref[...] = o.astype(o_ref.dtype)


def _mm(lhs, ws, epilogue, out_dtypes, n, *, rows=(), vecs=(), tm_target=1056, tn_target=512):
    m = lhs[0].shape[0]
    tm = _tile(m, tm_target, 16)
    tn = _tile(math.gcd(n, *[off for _, _, off in ws], *[off for _, off in rows]), tn_target, LANES)
    in_specs = [pl.BlockSpec((tm, a.shape[1]), lambda i, j: (i, 0)) for a in lhs]
    in_specs += [pl.BlockSpec((w.shape[0], tn), functools.partial(lambda i, j, o: (0, j + o), o=off // tn))
                 for w, _, off in ws]
    in_specs += [pl.BlockSpec((tm, tn), functools.partial(lambda i, j, o: (i, j + o), o=off // tn))
                 for _, off in rows]
    in_specs += [pl.BlockSpec((1, tn), lambda i, j: (0, j)) for _ in vecs]
    out_specs = [pl.BlockSpec((tm, tn), lambda i, j: (i, j)) for _ in out_dtypes]
    out_shape = [jax.ShapeDtypeStruct((m, n), dt) for dt in out_dtypes]
    kern = functools.partial(_mm_kernel, n_lhs=len(lhs), dots=tuple(li for _, li, _ in ws),
                             n_rows=len(rows), n_vecs=len(vecs), epilogue=epilogue)
    outs = pl.pallas_call(
        kern,
        grid=(m // tm, n // tn),
        in_specs=in_specs,
        out_specs=out_specs,
        out_shape=out_shape,
        compiler_params=_params(("parallel", "arbitrary")),
    )(*lhs, *[w for w, _, _ in ws], *[r for r, _ in rows], *vecs)
    return outs


def _head_rms(z, gain):
    parts = []
    for c in range(z.shape[1] // HEAD_DIM):
        zc = z[:, c * HEAD_DIM:(c + 1) * HEAD_DIM]
        y = zc * lax.rsqrt(jnp.mean(zc * zc, axis=-1, keepdims=True) + EPS)
        parts.append(y * gain[:, c * HEAD_DIM:(c + 1) * HEAD_DIM])
    return jnp.concatenate(parts, axis=1) if len(parts) > 1 else parts[0]


def _log_sigmoid(x):
    return jnp.minimum(x, 0.0) - jnp.log1p(jnp.exp(-jnp.abs(x)))


def _kv_proj_kernel(h_ref, w_ref, g_ref, ob_ref, of_ref, *, normalise):
    z = _mxu_dot(h_ref[...], w_ref[...])
    if normalise:
        z = _head_rms(z, g_ref[...])
    ob_ref[...] = z.astype(ob_ref.dtype)
    for hh in range(of_ref.shape[1]):
        of_ref[:, hh, :] = z[:, hh * HEAD_DIM:(hh + 1) * HEAD_DIM]


def _kv_proj(h, w, col_off, gain_tiled, n_heads, *, normalise):
    m, kdim = h.shape
    hb = min(n_heads, 8)
    tn = hb * HEAD_DIM
    tm = _tile(m, 1056, 16)
    assert col_off % tn == 0
    off = col_off // tn
    return pl.pallas_call(
        functools.partial(_kv_proj_kernel, normalise=normalise),
        grid=(m // tm, n_heads // hb),
        in_specs=[pl.BlockSpec((tm, kdim), lambda i, j: (i, 0)),
                  pl.BlockSpec((kdim, tn), lambda i, j: (0, j + off)),
                  pl.BlockSpec((1, tn), lambda i, j: (0, j))],
        out_specs=[pl.BlockSpec((tm, tn), lambda i, j: (i, j)),
                   pl.BlockSpec((tm, hb, HEAD_DIM), lambda i, j: (i, j, 0))],
        out_shape=[jax.ShapeDtypeStruct((m, n_heads * HEAD_DIM), BF16),
                   jax.ShapeDtypeStruct((m, n_heads, HEAD_DIM), F32)],
        compiler_params=_params(("parallel", "arbitrary")),
    )(h, w, gain_tiled)


def _cumsum_kernel(x_ref, o_ref):
    x = x_ref[0]
    s = x.shape[0]
    row = lax.broadcasted_iota(jnp.int32, x.shape, 0)
    k = 1
    while k < s:
        x = x + jnp.where(row >= k, pltpu.roll(x, k, 0), 0.0)
        k *= 2
    o_ref[0] = x


def _cumsum_seq(x):
    b, s, w = x.shape
    return pl.pallas_call(
        _cumsum_kernel,
        grid=(b,),
        in_specs=[pl.BlockSpec((1, s, w), lambda i: (i, 0, 0))],
        out_specs=pl.BlockSpec((1, s, w), lambda i: (i, 0, 0)),
        out_shape=jax.ShapeDtypeStruct((b, s, w), F32),
        compiler_params=_params(("parallel",)),
    )(x)


def _attn_prompt_kernel(q_ref, k_ref, v_ref, c_ref, ct_ref, o_ref, m_scr, l_scr, acc_scr, *, tq, hps, scale):
    hg = pl.program_id(1)
    qi = pl.program_id(2)
    m_scr[...] = jnp.full_like(m_scr, NEG_BIG)
    l_scr[...] = jnp.zeros_like(l_scr)
    acc_scr[...] = jnp.zeros_like(acc_scr)
    c_blk = c_ref[0]
    lane = lax.broadcasted_iota(jnp.int32, c_blk.shape, 1)
    c_cols = [jnp.sum(jnp.where(lane == hg * hps + hh, c_blk, 0.0), axis=1, keepdims=True) for hh in range(hps)]
    q_pos = qi * tq + lax.broadcasted_iota(jnp.int32, (tq, tq), 0)
    k_off = lax.broadcasted_iota(jnp.int32, (tq, tq), 1)

    def step(j, carry):
        start = pl.multiple_of(j * tq, tq)
        causal = j * tq + k_off <= q_pos
        for hh in range(hps):
            cols = slice(hh * HEAD_DIM, (hh + 1) * HEAD_DIM)
            kb = k_ref[0, pl.ds(start, tq), cols]
            vb = v_ref[0, pl.ds(start, tq), cols]
            c_row = ct_ref[0, pl.ds(hg * hps + hh, 1), pl.ds(start, tq)]
            s = lax.dot_general(q_ref[0, :, cols], kb, (((1,), (1,)), ((), ())),
                                preferred_element_type=F32) * scale
            s = jnp.where(causal, s + c_cols[hh] - c_row, NEG_BIG)
            m_old = m_scr[hh]
            m_new = jnp.maximum(m_old, jnp.max(s, axis=1, keepdims=True))
            p = jnp.exp(s - m_new)
            alpha = jnp.exp(m_old - m_new)
            l_scr[hh] = alpha * l_scr[hh] + jnp.sum(p, axis=1, keepdims=True)
            acc_scr[hh] = alpha * acc_scr[hh] + jnp.dot(p.astype(BF16), vb, preferred_element_type=F32)
            m_scr[hh] = m_new
        return carry

    lax.fori_loop(0, qi + 1, step, 0)
    for hh in range(hps):
        o_ref[0, :, hh * HEAD_DIM:(hh + 1) * HEAD_DIM] = (acc_scr[hh] / l_scr[hh]).astype(o_ref.dtype)


def _attn_prompt(q, k, v, c, c_t, n_heads):
    b, s, _ = q.shape
    tq = _tile(s, 512, LANES)
    hps = _tile(n_heads, 4, 1)
    w = hps * HEAD_DIM
    return pl.pallas_call(
        functools.partial(_attn_prompt_kernel, tq=tq, hps=hps, scale=HEAD_DIM ** -0.5),
        grid=(b, n_heads // hps, s // tq),
        in_specs=[pl.BlockSpec((1, tq, w), lambda bi, h, qi: (bi, qi, h)),
                  pl.BlockSpec((1, s, w), lambda bi, h, qi: (bi, 0, h)),
                  pl.BlockSpec((1, s, w), lambda bi, h, qi: (bi, 0, h)),
                  pl.BlockSpec((1, tq, LANES), lambda bi, h, qi: (bi, qi, 0)),
                  pl.BlockSpec((1, n_heads, s), lambda bi, h, qi: (bi, 0, 0))],
        out_specs=pl.BlockSpec((1, tq, w), lambda bi, h, qi: (bi, qi, h)),
        out_shape=jax.ShapeDtypeStruct(q.shape, BF16),
        scratch_shapes=[pltpu.VMEM((hps, tq, 1), F32),
                        pltpu.VMEM((hps, tq, 1), F32),
                        pltpu.VMEM((hps, tq, HEAD_DIM), F32)],
        compiler_params=_params(("parallel", "parallel", "arbitrary")),
    )(q, k, v, c, c_t)


def _attn_sample_kernel(pt_ref, q_ref, kn_ref, vn_ref, lfn_ref, lfnt_ref, *rest,
                        n_heads, npg, page, t_len, scale):
    k_refs = rest[:npg]
    v_refs = rest[npg:2 * npg]
    lf_refs = rest[2 * npg:3 * npg]
    o_ref = rest[3 * npg]
    m_scr, l_scr, acc_scr, s_scr, d_scr, carry_scr, ccol_scr = rest[3 * npg + 1:]
    g = pl.program_id(1)
    lane_h = lax.broadcasted_iota(jnp.int32, (n_heads, LANES), 1)

    def head_rows(hh):
        return slice(hh * t_len, (hh + 1) * t_len)

    def head_cols(hh):
        return slice(hh * HEAD_DIM, (hh + 1) * HEAD_DIM)

    def softmax_pv(width, v_of_head):
        s = s_scr[:, :width]
        m_old = m_scr[...]
        m_new = jnp.maximum(m_old, jnp.max(s, axis=1, keepdims=True))
        alpha = jnp.exp(m_old - m_new)
        p = jnp.exp(s - m_new)
        l_scr[...] = alpha * l_scr[...] + jnp.sum(p, axis=1, keepdims=True)
        m_scr[...] = m_new
        s_scr[:, :width] = p
        for hh in range(n_heads):
            r = head_rows(hh)
            pv = jnp.dot(s_scr[r, :width].astype(BF16), v_of_head(hh), preferred_element_type=F32)
            acc_scr[r, :] = alpha[r] * acc_scr[r, :] + pv

    @pl.when(g == 0)
    def _():
        sub = lax.broadcasted_iota(jnp.int32, (t_len, LANES), 0)
        lane_t = lax.broadcasted_iota(jnp.int32, (t_len, LANES), 1)
        cn = lfn_ref[0]
        k = 1
        while k < t_len:
            cn = cn + jnp.where(sub >= k, pltpu.roll(cn, k, 0), 0.0)
            k *= 2
        cnt = lfnt_ref[0]
        k = 1
        while k < t_len:
            cnt = cnt + jnp.where(lane_h >= k, pltpu.roll(cnt, k, 1), 0.0)
            k *= 2
        carry_scr[...] = jnp.zeros_like(carry_scr)
        m_scr[...] = jnp.full_like(m_scr, NEG_BIG)
        l_scr[...] = jnp.zeros_like(l_scr)
        acc_scr[...] = jnp.zeros_like(acc_scr)
        d_scr[:, :LANES] = cnt
        t_pos = lax.broadcasted_iota(jnp.int32, (t_len, page), 0)
        s_pos = lax.broadcasted_iota(jnp.int32, (t_len, page), 1)
        for hh in range(n_heads):
            r = head_rows(hh)
            c_col = jnp.sum(jnp.where(lane_t == hh, cn, 0.0), axis=1, keepdims=True)
            ccol_scr[r, :] = c_col
            s = lax.dot_general(q_ref[0, :, head_cols(hh)], kn_ref[0, :, head_cols(hh)],
                                (((1,), (1,)), ((), ())), preferred_element_type=F32) * scale
            s_scr[r, :page] = jnp.where(s_pos <= t_pos, s + c_col - d_scr[hh:hh + 1, :LANES], NEG_BIG)
        softmax_pv(page, lambda hh: vn_ref[0, :, head_cols(hh)])

    for pg in range(npg):
        lf = lf_refs[pg][0]
        sfx = lf
        k = 1
        while k < page:
            sfx = sfx + jnp.where(lane_h + k < page, pltpu.roll(sfx, page - k, 1), 0.0)
            k *= 2
        d_scr[:, pg * page:(pg + 1) * page] = sfx - lf + carry_scr[...]
        carry_scr[...] = carry_scr[...] + jnp.sum(lf, axis=1, keepdims=True)

    def gather_head(refs, hh):
        return jnp.concatenate([r[0, pl.ds(hh, page, stride=n_heads), :] for r in refs], axis=0).astype(BF16)

    for hh in range(n_heads):
        r = head_rows(hh)
        s = lax.dot_general(q_ref[0, :, head_cols(hh)], gather_head(k_refs, hh),
                            (((1,), (1,)), ((), ())), preferred_element_type=F32) * scale
        s_scr[r, :] = s + ccol_scr[r, :] + d_scr[hh:hh + 1, :]
    softmax_pv(npg * page, lambda hh: gather_head(v_refs, hh))

    @pl.when(g == pl.num_programs(1) - 1)
    def _():
        out = acc_scr[...] / l_scr[...]
        for hh in range(n_heads):
            o_ref[0, :, head_cols(hh)] = out[head_rows(hh), :].astype(o_ref.dtype)


def _attn_sample(q, k_new, v_new, lf_new, lf_new_t, cache_k, cache_v, cache_lf_t, page_table, n_heads):
    db, t_len, w = q.shape
    n_pages = page_table.shape[1]
    page = cache_lf_t.shape[2]
    assert page == LANES and t_len == 8
    npg = _tile(n_pages, 8, 1)
    hr = n_heads * t_len

    def page_map(b, g, pt, *, pg):
        return (pt[b, n_pages - 1 - (g * npg + pg)], 0, 0)

    def seq_map(b, g, pt):
        return (b, 0, 0)

    in_specs = [pl.BlockSpec((1, t_len, w), seq_map),
                pl.BlockSpec((1, page, w), seq_map),
                pl.BlockSpec((1, page, w), seq_map),
                pl.BlockSpec((1, t_len, LANES), seq_map),
                pl.BlockSpec((1, n_heads, LANES), seq_map)]
    in_specs += [pl.BlockSpec((1, page * n_heads, HEAD_DIM), functools.partial(page_map, pg=pg)) for pg in range(npg)]
    in_specs += [pl.BlockSpec((1, page * n_heads, HEAD_DIM), functools.partial(page_map, pg=pg)) for pg in range(npg)]
    in_specs += [pl.BlockSpec((1, n_heads, page), functools.partial(page_map, pg=pg)) for pg in range(npg)]
    kern = functools.partial(_attn_sample_kernel, n_heads=n_heads, npg=npg, page=page,
                             t_len=t_len, scale=HEAD_DIM ** -0.5)
    return pl.pallas_call(
        kern,
        grid_spec=pltpu.PrefetchScalarGridSpec(
            num_scalar_prefetch=1,
            grid=(db, n_pages // npg),
            in_specs=in_specs,
            out_specs=pl.BlockSpec((1, t_len, w), seq_map),
            scratch_shapes=[pltpu.VMEM((hr, 1), F32),
                            pltpu.VMEM((hr, 1), F32),
                            pltpu.VMEM((hr, HEAD_DIM), F32),
                            pltpu.VMEM((hr, npg * page), F32),
                            pltpu.VMEM((n_heads, npg * page), F32),
                            pltpu.VMEM((n_heads, 1), F32),
                            pltpu.VMEM((hr, 1), F32)]),
        out_shape=jax.ShapeDtypeStruct(q.shape, BF16),
        compiler_params=_params(("parallel", "arbitrary")),
    )(page_table, q, k_new, v_new, lf_new, lf_new_t,
      *([cache_k] * npg), *([cache_v] * npg), *([cache_lf_t] * npg))


def _s5_kernel(u_ref, bb_ref, cc_ref, a_ref, d_ref, init_ref, yg_ref, fin_ref, lhs_scr, x_scr, y_scr, st_scr,
               *, n_real, nb, tc, half, nsub, exact, unroll):
    tci = pl.program_id(1)

    @pl.when(tci == 0)
    def _():
        st_scr[...] = init_ref[...]
        lhs_scr[...] = jnp.zeros_like(lhs_scr)

    def dot(a, b):
        if exact:
            return jnp.dot(a, b, precision=lax.Precision.HIGHEST, preferred_element_type=F32)
        return jnp.dot(a, b, preferred_element_type=F32)

    for s in range(nsub):
        for b in range(n_real):
            lhs_scr[s, pl.ds(b, tc, stride=nb), :] = u_ref[b, :, s * LANES:(s + 1) * LANES]
        x_scr[s] = dot(lhs_scr[s], bb_ref[s])
    a_re = [a_ref[s, :, :half] for s in range(nsub)]
    a_im = [a_ref[s, :, half:] for s in range(nsub)]

    def step(t, carry):
        rows = pl.ds(pl.multiple_of(t * nb, nb), nb)
        out = []
        for s in range(nsub):
            xr, xi = carry[2 * s], carry[2 * s + 1]
            nr = a_re[s] * xr - a_im[s] * xi + x_scr[s, rows, :half]
            ni = a_re[s] * xi + a_im[s] * xr + x_scr[s, rows, half:]
            x_scr[s, rows, :half] = nr
            x_scr[s, rows, half:] = ni
            out += [nr, ni]
        return tuple(out)

    init = []
    for s in range(nsub):
        init += [st_scr[s, :, :half], st_scr[s, :, half:]]
    fin = lax.fori_loop(0, tc, step, tuple(init), unroll=unroll)
    for s in range(nsub):
        st_scr[s, :, :half] = fin[2 * s]
        st_scr[s, :, half:] = fin[2 * s + 1]
        y = dot(x_scr[s], cc_ref[s]) + d_ref[:, s * LANES:(s + 1) * LANES] * lhs_scr[s]
        y_scr[s] = jax.nn.gelu(y, approximate=True)
        for b in range(n_real):
            yg_ref[b, :, s * LANES:(s + 1) * LANES] = y_scr[s, pl.ds(b, tc, stride=nb), :]

    @pl.when(tci == pl.num_programs(1) - 1)
    def _():
        fin_ref[...] = st_scr[...]


def _s5(u, bb, cc, abar, d, init, *, exact):
    n_real, t_len, w = u.shape
    nj, nb, two_half = init.shape
    half = two_half // 2
    nsub = _tile(nj, max(1, 16 // nb), 1)
    cw = nsub * LANES
    tc = _tile(t_len, max(8, 1024 // nb), 8)
    kern = functools.partial(_s5_kernel, n_real=n_real, nb=nb, tc=tc, half=half, nsub=nsub, exact=exact,
                             unroll=min(8, tc))
    return pl.pallas_call(
        kern,
        grid=(nj // nsub, t_len // tc),
        in_specs=[pl.BlockSpec((n_real, tc, cw), lambda j, t: (0, t, j)),
                  pl.BlockSpec((nsub, LANES, two_half), lambda j, t: (j, 0, 0)),
                  pl.BlockSpec((nsub, two_half, LANES), lambda j, t: (j, 0, 0)),
                  pl.BlockSpec((nsub, 1, two_half), lambda j, t: (j, 0, 0)),
                  pl.BlockSpec((1, cw), lambda j, t: (0, j)),
                  pl.BlockSpec((nsub, nb, two_half), lambda j, t: (j, 0, 0))],
        out_specs=[pl.BlockSpec((n_real, tc, cw), lambda j, t: (0, t, j)),
                   pl.BlockSpec((nsub, nb, two_half), lambda j, t: (j, 0, 0))],
        out_shape=[jax.ShapeDtypeStruct((n_real, t_len, w), F32),
                   jax.ShapeDtypeStruct((nj, nb, two_half), F32)],
        scratch_shapes=[pltpu.VMEM((nsub, tc * nb, LANES), F32),
                        pltpu.VMEM((nsub, tc * nb, two_half), F32),
                        pltpu.VMEM((nsub, tc * nb, LANES), F32),
                        pltpu.VMEM((nsub, nb, two_half), F32)],
        compiler_params=_params(("parallel", "arbitrary")),
    )(u, bb, cc, abar, d.reshape(1, w), init)


def _s5_params(a_re, a_im, log_dt, b_re, b_im, c_re, c_im):
    g, p = a_re.shape
    ch = b_re.shape[2]
    gpb = LANES // ch
    nj = g // gpb
    dt = jnp.exp(log_dt)[:, None]
    mag = jnp.exp(dt * a_re)
    ang = dt * a_im
    abar_re = mag * jnp.cos(ang)
    abar_im = mag * jnp.sin(ang)
    den = a_re * a_re + a_im * a_im
    nr = abar_re - 1.0
    coef_re = (nr * a_re + abar_im * a_im) / den
    coef_im = (abar_im * a_re - nr * a_im) / den
    bb_re = coef_re[..., None] * b_re - coef_im[..., None] * b_im
    bb_im = coef_re[..., None] * b_im + coef_im[..., None] * b_re
    eye = jnp.eye(gpb, dtype=F32)

    def pack_in(m):
        m = m.reshape(nj, gpb, p, ch).transpose(0, 1, 3, 2)
        return jnp.einsum('jgcp,gh->jgchp', m, eye).reshape(nj, gpb * ch, gpb * p)

    def pack_out(m):
        m = m.reshape(nj, gpb, ch, p).transpose(0, 1, 3, 2)
        return jnp.einsum('jgpc,gh->jgphc', m, eye).reshape(nj, gpb * p, gpb * ch)

    bb = jnp.concatenate([pack_in(bb_re), pack_in(bb_im)], axis=2)
    cc = jnp.concatenate([pack_out(c_re), pack_out(-c_im)], axis=1)
    abar = jnp.concatenate([abar_re.reshape(nj, 1, gpb * p), abar_im.reshape(nj, 1, gpb * p)], axis=2)
    return bb, cc, abar


def _pack_state(s_re, s_im, nj):
    nb = s_re.shape[0]
    return jnp.concatenate([s_re.reshape(nb, nj, -1).transpose(1, 0, 2),
                            s_im.reshape(nb, nj, -1).transpose(1, 0, 2)], axis=2)


def _unpack_state(fin, n_real, g, p):
    nj, nb, two_half = fin.shape
    half = two_half // 2
    re = fin[:, :n_real, :half].transpose(1, 0, 2).reshape(n_real, g, p)
    im = fin[:, :n_real, half:].transpose(1, 0, 2).reshape(n_real, g, p)
    return re, im


def kernel(x_prompt, x_sample, cache_k, cache_v, cache_logf, state_ssm_re, state_ssm_im, page_table, p_prompt, p_sample, ffn1_norm, ffn1_w_gate, ffn1_w_up, ffn1_w_down, mix_norm, w_in, b_f, q_norm, k_norm, ssm_a_re, ssm_a_im, ssm_log_dt, ssm_b_re, ssm_b_im, ssm_c_re, ssm_c_im, ssm_d, ssm_w_glu, w_branch_a, w_branch_b, w_out, ffn2_norm, ffn2_w_gate, ffn2_w_up, ffn2_w_down, ple_norm, w_ple, w_ple_gate):
    depth = ffn1_norm.shape[0]
    assert depth == 1
    bsz, seq, dm = x_prompt.shape
    dbsz, dseq, _ = x_sample.shape
    n_heads = cache_k.shape[3]
    attn_w = n_heads * HEAD_DIM
    n_groups, ssm_p = ssm_a_re.shape[1], ssm_a_re.shape[2]
    ssm_w = n_groups * SSM_CH
    n_phys, page = cache_k.shape[1], cache_k.shape[2]
    mp = bsz * seq
    ms = dbsz * dseq
    o3 = 3 * attn_w
    o4 = o3 + n_heads

    x = jnp.concatenate([x_prompt.reshape(mp, dm), x_sample.reshape(ms, dm)], axis=0)
    pe_in = jnp.concatenate([p_prompt[0].reshape(mp, -1), p_sample[0].reshape(ms, -1)], axis=0).astype(BF16)

    x1 = _ffn(x, ffn1_norm[0], ffn1_w_gate[0], ffn1_w_up[0], ffn1_w_down[0])

    h2 = _rmsnorm(x1, mix_norm[0])
    w_in0 = w_in[0]
    w_qkv = w_in0[:, :o3].astype(BF16)
    w_rest = w_in0[:, o4:].astype(BF16)
    w_f = jnp.pad(w_in0[:, o3:o4], ((0, 0), (0, LANES - n_heads)))
    b_f_pad = jnp.pad(b_f[0], (0, LANES - n_heads)).reshape(1, LANES)
    qg = jnp.tile(q_norm[0], n_heads).reshape(1, attn_w)
    kg = jnp.tile(k_norm[0], n_heads).reshape(1, attn_w)
    (q,) = _mm([h2], [(w_qkv, 0, 0)], lambda a, r, c: [_head_rms(a[0], c[0])], [BF16], attn_w, vecs=[qg])
    kb, k_out = _kv_proj(h2, w_qkv, attn_w, kg, n_heads, normalise=True)
    vb, v_out = _kv_proj(h2, w_qkv, 2 * attn_w, kg, n_heads, normalise=False)
    (logf,) = _mm([h2], [(w_f, 0, 0)], lambda a, r, c: [_log_sigmoid(a[0] + c[0])], [F32], LANES, vecs=[b_f_pad])
    (u,) = _mm([h2], [(w_rest, 0, 0)], lambda a, r, c: [a[0]], [F32], ssm_w)
    gate_a, gate_b = _mm([h2], [(w_rest, 0, ssm_w), (w_rest, 0, ssm_w + dm)],
                         lambda a, r, c: [jax.nn.sigmoid(a[0]), jax.nn.sigmoid(a[1])], [F32, F32], dm)

    lf_p = logf[:mp].reshape(bsz, seq, LANES)
    c_p = _cumsum_seq(lf_p)
    c_p_t = c_p[:, :, :n_heads].transpose(0, 2, 1)
    a_p = _attn_prompt(q[:mp].reshape(bsz, seq, attn_w), kb[:mp].reshape(bsz, seq, attn_w),
                       vb[:mp].reshape(bsz, seq, attn_w), c_p, c_p_t, n_heads)

    lf_s = logf[mp:].reshape(dbsz, dseq, LANES)
    lf_s_t = jnp.pad(lf_s[:, :, :n_heads].transpose(0, 2, 1), ((0, 0), (0, 0), (0, LANES - dseq)))
    pad_new = ((0, 0), (0, page - dseq), (0, 0))
    k_new = jnp.pad(kb[mp:].reshape(dbsz, dseq, attn_w), pad_new)
    v_new = jnp.pad(vb[mp:].reshape(dbsz, dseq, attn_w), pad_new)
    a_s = _attn_sample(q[mp:].reshape(dbsz, dseq, attn_w), k_new, v_new, lf_s, lf_s_t,
                       cache_k[0].reshape(n_phys, page * n_heads, HEAD_DIM),
                       cache_v[0].reshape(n_phys, page * n_heads, HEAD_DIM),
                       cache_logf[0].transpose(0, 2, 1), page_table, n_heads)
    a = jnp.concatenate([a_p.reshape(mp, attn_w), a_s.reshape(ms, attn_w)], axis=0)

    bb, cc, abar = _s5_params(ssm_a_re[0], ssm_a_im[0], ssm_log_dt[0], ssm_b_re[0], ssm_b_im[0],
                              ssm_c_re[0], ssm_c_im[0])
    nj = bb.shape[0]
    d_vec = ssm_d[0].reshape(ssm_w)
    nb_p = -(-bsz // 8) * 8
    init_p = jnp.zeros((nj, nb_p, bb.shape[2]), F32)
    yg_p, fin_p = _s5(u[:mp].reshape(bsz, seq, ssm_w), bb, cc, abar, d_vec, init_p, exact=False)
    nb_s = -(-dbsz // 8) * 8
    pad_state = ((0, nb_s - dbsz), (0, 0), (0, 0))
    init_s = _pack_state(jnp.pad(state_ssm_re[0], pad_state), jnp.pad(state_ssm_im[0], pad_state), nj)
    yg_s, fin_s = _s5(u[mp:].reshape(dbsz, dseq, ssm_w), bb, cc, abar, d_vec, init_s, exact=True)
    yg = jnp.concatenate([yg_p.reshape(mp, ssm_w), yg_s.reshape(ms, ssm_w)], axis=0)
    (b_out,) = _mm([yg], [(ssm_w_glu[0], 0, 0)], lambda a, r, c: [r[0] * jax.nn.sigmoid(a[0])], [BF16], ssm_w,
                   rows=[(yg, 0)], tm_target=528)

    (merged,) = _mm([a, b_out], [(w_branch_a[0], 0, 0), (w_branch_b[0], 1, 0)],
                    lambda acc, r, c: [r[0] * acc[0] + r[1] * acc[1]], [BF16], dm,
                    rows=[(gate_a, 0), (gate_b, 0)])
    (x2,) = _mm([merged], [(w_out[0], 0, 0)], lambda acc, r, c: [r[0] + acc[0]], [F32], dm, rows=[(x1, 0)])

    x3 = _ffn(x2, ffn2_norm[0], ffn2_w_gate[0], ffn2_w_up[0], ffn2_w_down[0])

    h4 = _rmsnorm(x3, ple_norm[0])
    (x4,) = _mm([pe_in, h4], [(w_ple[0], 0, 0), (w_ple_gate[0], 1, 0)],
                lambda acc, r, c: [r[0] + acc[0] * jax.nn.sigmoid(acc[1])], [F32], dm, rows=[(x3, 0)])

    sre_p, sim_p = _unpack_state(fin_p, bsz, n_groups, ssm_p)
    sre_s, sim_s = _unpack_state(fin_s, dbsz, n_groups, ssm_p)
    return (x4[:mp].reshape(bsz, seq, dm), x4[mp:].reshape(dbsz, dseq, dm),
            k_out[:mp].reshape(1, bsz, seq, n_heads, HEAD_DIM), v_out[:mp].reshape(1, bsz, seq, n_heads, HEAD_DIM),
            lf_p[:, :, :n_heads].reshape(1, bsz, seq, n_heads),
            sre_p[None], sim_p[None],
            k_out[mp:].reshape(1, dbsz, dseq, n_heads, HEAD_DIM), v_out[mp:].reshape(1, dbsz, dseq, n_heads, HEAD_DIM),
            lf_s[:, :, :n_heads].reshape(1, dbsz, dseq, n_heads),
            sre_s[None], sim_s[None])
```

```python
import functools
import math

import jax
import jax.numpy as jnp
from jax import lax
from jax.experimental import pallas as pl
from jax.experimental.pallas import tpu as pltpu

F32 = jnp.float32
BF16 = jnp.bfloat16

EPS = 1e-6
HEAD_DIM = 128
SSM_CH = 16
LANES = 128
VMEM_LIMIT_BYTES = 56 * 1024 * 1024
NEG_BIG = -1e30


def _params(sem):
    return pltpu.CompilerParams(dimension_semantics=sem, vmem_limit_bytes=VMEM_LIMIT_BYTES)


def _tile(n, target, mult):
    best = None
    for t in range(mult, min(n, target) + 1, mult):
        if n % t == 0:
            best = t
    assert best is not None, (n, target, mult)
    return best


def _ffn_kernel(*refs, tm, rc, rows_a, two_src):
    if two_src:
        xa_hbm, xb_hbm = refs[:2]
        refs = refs[2:]
    else:
        xa_hbm = refs[0]
        refs = refs[1:]
    g_ref, g2_ref, wg_ref, wu_ref, wd_ref, o_hbm, h_hbm, acc, h, sem = refs
    i = pl.program_id(0)
    k = pl.program_id(1)
    n_full = rows_a // tm
    part = rows_a - n_full * tm

    def norm_rows(gain_ref):
        def body(r, carry):
            rows = pl.ds(pl.multiple_of(r * rc, rc), rc)
            xs = acc[rows, :]
            y = xs * lax.rsqrt(jnp.mean(xs * xs, axis=-1, keepdims=True) + EPS)
            h[rows, :] = (y * gain_ref[...]).astype(BF16)
            return carry
        lax.fori_loop(0, tm // rc, body, 0)

    @pl.when(k == 0)
    def _():
        @pl.when(i < n_full)
        def _():
            cp = pltpu.make_async_copy(xa_hbm.at[pl.ds(i * tm, tm)], acc, sem.at[0])
            cp.start()
            cp.wait()

        if two_src:
            @pl.when(i == n_full)
            def _():
                cps = [pltpu.make_async_copy(xb_hbm.at[pl.ds(0, tm - part)], acc.at[pl.ds(part, tm - part)], sem.at[1])]
                if part:
                    cps.append(pltpu.make_async_copy(xa_hbm.at[pl.ds(n_full * tm, part)], acc.at[pl.ds(0, part)],
                                                     sem.at[0]))
                for cp in cps:
                    cp.start()
                for cp in cps:
                    cp.wait()

        norm_rows(g_ref)

    hb = h[...]
    g = jnp.dot(hb, wg_ref[...].astype(BF16), preferred_element_type=F32)
    u = jnp.dot(hb, wu_ref[...].astype(BF16), preferred_element_type=F32)
    a = (0.5 * (g * jax.nn.sigmoid(g)) * u).astype(BF16)
    acc[...] += jnp.dot(a, wd_ref[...].astype(BF16), preferred_element_type=F32)

    @pl.when(k == pl.num_programs(1) - 1)
    def _():
        cp_x = pltpu.make_async_copy(acc, o_hbm.at[pl.ds(i * tm, tm)], sem.at[0])
        cp_x.start()
        norm_rows(g2_ref)
        cp_h = pltpu.make_async_copy(h, h_hbm.at[pl.ds(i * tm, tm)], sem.at[1])
        cp_h.start()
        cp_x.wait()
        cp_h.wait()


def _ffn(xs, gain, w_gate, w_up, w_down, next_gain):
    d = xs[0].shape[1]
    m = sum(x.shape[0] for x in xs)
    dff = w_gate.shape[1]
    tm = _tile(m, 768, 16)
    tf = _tile(dff, 256, LANES)
    rc = _tile(tm, 64, 8)
    rows_a = xs[0].shape[0]
    two_src = len(xs) == 2
    if two_src:
        assert (rows_a // tm + 1) * tm == m
    else:
        assert rows_a == m
    return pl.pallas_call(
        functools.partial(_ffn_kernel, tm=tm, rc=rc, rows_a=rows_a, two_src=two_src),
        grid=(m // tm, dff // tf),
        in_specs=[pl.BlockSpec(memory_space=pl.ANY)] * len(xs) + [
                  pl.BlockSpec((1, d), lambda i, k: (0, 0)),
                  pl.BlockSpec((1, d), lambda i, k: (0, 0)),
                  pl.BlockSpec((d, tf), lambda i, k: (0, k)),
                  pl.BlockSpec((d, tf), lambda i, k: (0, k)),
                  pl.BlockSpec((tf, d), lambda i, k: (k, 0))],
        out_specs=[pl.BlockSpec(memory_space=pl.ANY), pl.BlockSpec(memory_space=pl.ANY)],
        out_shape=[jax.ShapeDtypeStruct((m, d), F32), jax.ShapeDtypeStruct((m, d), BF16)],
        scratch_shapes=[pltpu.VMEM((tm, d), F32),
                        pltpu.VMEM((tm, d), BF16),
                        pltpu.SemaphoreType.DMA((2,))],
        compiler_params=_params(("arbitrary", "arbitrary")),
    )(*xs, gain.reshape(1, d), next_gain.reshape(1, d), w_gate, w_up, w_down)


def _mxu_dot(lhs, w, transposed=False):
    if lhs.dtype == BF16 and w.dtype != BF16:
        w = w.astype(BF16)
    if transposed:
        return lax.dot_general(lhs, w, (((1,), (1,)), ((), ())), preferred_element_type=F32)
    return jnp.dot(lhs, w, preferred_element_type=F32)


def _mm_kernel(*refs, n_lhs, dots, n_rows, n_vecs, epilogue):
    lhs_refs = refs[:n_lhs]
    w_refs = refs[n_lhs:n_lhs + len(dots)]
    p = n_lhs + len(dots)
    row_refs = refs[p:p + n_rows]
    vec_refs = refs[p + n_rows:p + n_rows + n_vecs]
    out_refs = refs[p + n_rows + n_vecs:]
    lhs_vals = [r[...] for r in lhs_refs]
    accs = [_mxu_dot(lhs_vals[li], w_ref[...], tr) for (li, tr), w_ref in zip(dots, w_refs)]
    outs = epilogue(accs, [r[...] for r in row_refs], [v[...] for v in vec_refs])
    for o_ref, o in zip(out_refs, outs):
        o_ref[...] = o.astype(o_ref.dtype)


def _mm(lhs, ws, epilogue, out_dtypes, n, *, rows=(), vecs=(), tm_target=1056, tn_target=512, row_off=0, m=None):
    if m is None:
        m = lhs[0].shape[0]
    tm = _tile(math.gcd(m, row_off), tm_target, 16)
    tn = _tile(math.gcd(n, *[off for _, _, off, _ in ws], *[off for _, off in rows]), tn_target, LANES)
    ro = row_off // tm
    in_specs = [pl.BlockSpec((tm, a.shape[1]), lambda i, j: (i + ro, 0)) for a in lhs]
    for w, _, off, tr in ws:
        if tr:
            in_specs.append(pl.BlockSpec((tn, w.shape[1]), functools.partial(lambda i, j, o: (j + o, 0), o=off // tn)))
        else:
            in_specs.append(pl.BlockSpec((w.shape[0], tn), functools.partial(lambda i, j, o: (0, j + o), o=off // tn)))
    in_specs += [pl.BlockSpec((tm, tn), functools.partial(lambda i, j, o: (i + ro, j + o), o=off // tn))
                 for _, off in rows]
    in_specs += [pl.BlockSpec((1, tn), lambda i, j: (0, j)) for _ in vecs]
    out_specs = [pl.BlockSpec((tm, tn), lambda i, j: (i, j)) for _ in out_dtypes]
    out_shape = [jax.ShapeDtypeStruct((m, n), dt) for dt in out_dtypes]
    kern = functools.partial(_mm_kernel, n_lhs=len(lhs), dots=tuple((li, tr) for _, li, _, tr in ws),
                             n_rows=len(rows), n_vecs=len(vecs), epilogue=epilogue)
    outs = pl.pallas_call(
        kern,
        grid=(m // tm, n // tn),
        in_specs=in_specs,
        out_specs=out_specs,
        out_shape=out_shape,
        compiler_params=_params(("parallel", "arbitrary")),
    )(*lhs, *[w for w, _, _, _ in ws], *[r for r, _ in rows], *vecs)
    return outs


def _head_rms(z, gain):
    parts = []
    for c in range(z.shape[1] // HEAD_DIM):
        zc = z[:, c * HEAD_DIM:(c + 1) * HEAD_DIM]
        y = zc * lax.rsqrt(jnp.mean(zc * zc, axis=-1, keepdims=True) + EPS)
        parts.append(y * gain[:, c * HEAD_DIM:(c + 1) * HEAD_DIM])
    return jnp.concatenate(parts, axis=1) if len(parts) > 1 else parts[0]


def _log_sigmoid(x):
    return jnp.minimum(x, 0.0) - jnp.log1p(jnp.exp(-jnp.abs(x)))


def _kv_proj_kernel(h_ref, w_ref, g_ref, ob_ref, of_ref, *, normalise):
    z = _mxu_dot(h_ref[...], w_ref[...], transposed=True)
    if normalise:
        z = _head_rms(z, g_ref[...])
    ob_ref[...] = z.astype(ob_ref.dtype)
    for hh in range(of_ref.shape[1]):
        of_ref[:, hh, :] = z[:, hh * HEAD_DIM:(hh + 1) * HEAD_DIM]


def _kv_proj(h, w_t, col_off, gain_tiled, n_heads, *, normalise, row_off, m):
    kdim = h.shape[1]
    hb = min(n_heads, 8)
    tn = hb * HEAD_DIM
    tm = _tile(math.gcd(m, row_off), 1056, 16)
    assert col_off % tn == 0
    off = col_off // tn
    ro = row_off // tm
    return pl.pallas_call(
        functools.partial(_kv_proj_kernel, normalise=normalise),
        grid=(m // tm, n_heads // hb),
        in_specs=[pl.BlockSpec((tm, kdim), lambda i, j: (i + ro, 0)),
                  pl.BlockSpec((tn, kdim), lambda i, j: (j + off, 0)),
                  pl.BlockSpec((1, tn), lambda i, j: (0, j))],
        out_specs=[pl.BlockSpec((tm, tn), lambda i, j: (i, j)),
                   pl.BlockSpec((tm, hb, HEAD_DIM), lambda i, j: (i, j, 0))],
        out_shape=[jax.ShapeDtypeStruct((m, n_heads * HEAD_DIM), BF16),
                   jax.ShapeDtypeStruct((m, n_heads, HEAD_DIM), F32)],
        compiler_params=_params(("parallel", "arbitrary")),
    )(h, w_t, gain_tiled)


def _cumsum_kernel(x_ref, o_ref):
    x = x_ref[0]
    s = x.shape[0]
    row = lax.broadcasted_iota(jnp.int32, x.shape, 0)
    k = 1
    while k < s:
        x = x + jnp.where(row >= k, pltpu.roll(x, k, 0), 0.0)
        k *= 2
    o_ref[0] = x


def _cumsum_seq(x):
    b, s, w = x.shape
    return pl.pallas_call(
        _cumsum_kernel,
        grid=(b,),
        in_specs=[pl.BlockSpec((1, s, w), lambda i: (i, 0, 0))],
        out_specs=pl.BlockSpec((1, s, w), lambda i: (i, 0, 0)),
        out_shape=jax.ShapeDtypeStruct((b, s, w), F32),
        compiler_params=_params(("parallel",)),
    )(x)


def _attn_prompt_kernel(q_ref, k_ref, v_ref, c_ref, ct_ref, o_ref, m_scr, l_scr, acc_scr, *, tq, hps, scale):
    hg = pl.program_id(1)
    qi = pl.program_id(2)
    m_scr[...] = jnp.full_like(m_scr, NEG_BIG)
    l_scr[...] = jnp.zeros_like(l_scr)
    acc_scr[...] = jnp.zeros_like(acc_scr)
    c_blk = c_ref[0]
    lane = lax.broadcasted_iota(jnp.int32, c_blk.shape, 1)
    c_cols = [jnp.sum(jnp.where(lane == hg * hps + hh, c_blk, 0.0), axis=1, keepdims=True) for hh in range(hps)]
    q_pos = qi * tq + lax.broadcasted_iota(jnp.int32, (tq, tq), 0)
    k_off = lax.broadcasted_iota(jnp.int32, (tq, tq), 1)

    def step(j, carry):
        start = pl.multiple_of(j * tq, tq)
        causal = j * tq + k_off <= q_pos
        for hh in range(hps):
            cols = slice(hh * HEAD_DIM, (hh + 1) * HEAD_DIM)
            kb = k_ref[pl.ds(start, tq), cols]
            vb = v_ref[pl.ds(start, tq), cols]
            c_row = ct_ref[0, pl.ds(hg * hps + hh, 1), pl.ds(start, tq)]
            s = lax.dot_general(q_ref[:, cols], kb, (((1,), (1,)), ((), ())),
                                preferred_element_type=F32) * scale
            s = jnp.where(causal, s + c_cols[hh] - c_row, NEG_BIG)
            m_old = m_scr[hh]
            m_new = jnp.maximum(m_old, jnp.max(s, axis=1, keepdims=True))
            p = jnp.exp(s - m_new)
            alpha = jnp.exp(m_old - m_new)
            l_scr[hh] = alpha * l_scr[hh] + jnp.sum(p, axis=1, keepdims=True)
            acc_scr[hh] = alpha * acc_scr[hh] + jnp.dot(p.astype(BF16), vb, preferred_element_type=F32)
            m_scr[hh] = m_new
        return carry

    lax.fori_loop(0, qi + 1, step, 0)
    for hh in range(hps):
        o_ref[:, hh * HEAD_DIM:(hh + 1) * HEAD_DIM] = (acc_scr[hh] / l_scr[hh]).astype(o_ref.dtype)


def _attn_prompt(q, k, v, c, c_t, n_heads):
    b, s, _ = c.shape
    tq = _tile(s, 512, LANES)
    hps = _tile(n_heads, 4, 1)
    w = hps * HEAD_DIM
    nq = s // tq
    return pl.pallas_call(
        functools.partial(_attn_prompt_kernel, tq=tq, hps=hps, scale=HEAD_DIM ** -0.5),
        grid=(b, n_heads // hps, nq),
        in_specs=[pl.BlockSpec((tq, w), lambda bi, h, qi: (bi * nq + qi, h)),
                  pl.BlockSpec((s, w), lambda bi, h, qi: (bi, h)),
                  pl.BlockSpec((s, w), lambda bi, h, qi: (bi, h)),
                  pl.BlockSpec((1, tq, LANES), lambda bi, h, qi: (bi, qi, 0)),
                  pl.BlockSpec((1, n_heads, s), lambda bi, h, qi: (bi, 0, 0))],
        out_specs=pl.BlockSpec((tq, w), lambda bi, h, qi: (bi * nq + qi, h)),
        out_shape=jax.ShapeDtypeStruct((b * s, q.shape[1]), BF16),
        scratch_shapes=[pltpu.VMEM((hps, tq, 1), F32),
                        pltpu.VMEM((hps, tq, 1), F32),
                        pltpu.VMEM((hps, tq, HEAD_DIM), F32)],
        compiler_params=_params(("parallel", "parallel", "arbitrary")),
    )(q, k, v, c, c_t)


def _attn_sample_kernel(pt_ref, q_ref, kn_ref, vn_ref, lfn_ref, lfnt_ref, *rest,
                        n_heads, npg, page, t_len, scale):
    k_refs = rest[:npg]
    v_refs = rest[npg:2 * npg]
    lf_refs = rest[2 * npg:3 * npg]
    o_ref = rest[3 * npg]
    m_scr, l_scr, acc_scr, s_scr, d_scr, carry_scr, ccol_scr = rest[3 * npg + 1:]
    g = pl.program_id(1)
    n_pairs = n_heads // 2
    pr = 2 * t_len
    w2 = 2 * page
    lane_p = lax.broadcasted_iota(jnp.int32, (n_pairs, LANES), 1)
    even_p = (lane_p & 1) == 0
    row_half = lax.broadcasted_iota(jnp.int32, (pr, LANES), 0) // t_len

    def pair_rows(p):
        return slice(p * pr, (p + 1) * pr)

    def head_cols(hh):
        return slice(hh * HEAD_DIM, (hh + 1) * HEAD_DIM)

    def softmax_pv(width, pv_of_pair):
        s = s_scr[:, :width]
        m_old = m_scr[...]
        m_new = jnp.maximum(m_old, jnp.max(s, axis=1, keepdims=True))
        alpha = jnp.exp(m_old - m_new)
        p = jnp.exp(s - m_new)
        l_scr[...] = alpha * l_scr[...] + jnp.sum(p, axis=1, keepdims=True)
        m_scr[...] = m_new
        s_scr[:, :width] = p
        for pp in range(n_pairs):
            r = pair_rows(pp)
            acc_scr[r, :] = alpha[r] * acc_scr[r, :] + pv_of_pair(pp, s_scr[r, :width].astype(BF16))

    @pl.when(g == 0)
    def _():
        sub = lax.broadcasted_iota(jnp.int32, (t_len, LANES), 0)
        lane_t = lax.broadcasted_iota(jnp.int32, (t_len, LANES), 1)
        lane_h = lax.broadcasted_iota(jnp.int32, (n_heads, LANES), 1)
        cn = lfn_ref[0]
        k = 1
        while k < t_len:
            cn = cn + jnp.where(sub >= k, pltpu.roll(cn, k, 0), 0.0)
            k *= 2
        cnt = lfnt_ref[0]
        k = 1
        while k < t_len:
            cnt = cnt + jnp.where(lane_h >= k, pltpu.roll(cnt, k, 1), 0.0)
            k *= 2
        carry_scr[...] = jnp.zeros_like(carry_scr)
        m_scr[...] = jnp.full_like(m_scr, NEG_BIG)
        l_scr[...] = jnp.zeros_like(l_scr)
        acc_scr[...] = jnp.zeros_like(acc_scr)
        t_pos = lax.broadcasted_iota(jnp.int32, (pr, page), 0) % t_len
        s_pos = lax.broadcasted_iota(jnp.int32, (pr, page), 1)
        for pp in range(n_pairs):
            r = pair_rows(pp)
            h0, h1 = pp, pp + n_pairs
            for half, hh in ((0, h0), (1, h1)):
                ccol_scr[pp * pr + half * t_len:pp * pr + (half + 1) * t_len, :] = jnp.sum(
                    jnp.where(lane_t == hh, cn, 0.0), axis=1, keepdims=True)
            q2 = q_ref[0, r, :]
            s0 = lax.dot_general(q2, kn_ref[0, :, head_cols(h0)], (((1,), (1,)), ((), ())),
                                 preferred_element_type=F32)
            s1 = lax.dot_general(q2, kn_ref[0, :, head_cols(h1)], (((1,), (1,)), ((), ())),
                                 preferred_element_type=F32)
            c_row = jnp.where(row_half == 0, cnt[h0:h0 + 1, :], cnt[h1:h1 + 1, :])
            s = jnp.where(row_half == 0, s0, s1) * scale + ccol_scr[r, :] - c_row
            s_scr[r, :page] = jnp.where(s_pos <= t_pos, s, NEG_BIG)

        def pv_new(pp, p2):
            pv0 = jnp.dot(p2, vn_ref[0, :, head_cols(pp)], preferred_element_type=F32)
            pv1 = jnp.dot(p2, vn_ref[0, :, head_cols(pp + n_pairs)], preferred_element_type=F32)
            return jnp.where(row_half == 0, pv0, pv1)

        softmax_pv(page, pv_new)

    def sfx_par(x):
        k = 2
        while k < LANES:
            x = x + jnp.where(lane_p + k < LANES, pltpu.roll(x, LANES - k, 1), 0.0)
            k *= 2
        return x

    def tot_par(x):
        te = jnp.sum(jnp.where(even_p, x, 0.0), axis=1, keepdims=True)
        to = jnp.sum(jnp.where(even_p, 0.0, x), axis=1, keepdims=True)
        return jnp.where(even_p, te, to)

    for pg in range(npg):
        lo = lf_refs[pg][0, :, :LANES]
        hi = lf_refs[pg][0, :, LANES:]
        tot_hi = tot_par(hi)
        carry = carry_scr[...]
        d_scr[:, pg * w2:pg * w2 + LANES] = sfx_par(lo) + tot_hi - lo + carry
        d_scr[:, pg * w2 + LANES:(pg + 1) * w2] = sfx_par(hi) - hi + carry
        carry_scr[...] = carry + tot_par(lo) + tot_hi

    def gather_pair(refs, pp):
        return jnp.concatenate([r[0, pl.ds(pp, w2, stride=n_pairs), :] for r in refs], axis=0).astype(BF16)

    width = npg * w2
    col_half = lax.broadcasted_iota(jnp.int32, (pr, width), 1) & 1
    same_head = col_half == lax.broadcasted_iota(jnp.int32, (pr, width), 0) // t_len
    for pp in range(n_pairs):
        r = pair_rows(pp)
        s = lax.dot_general(q_ref[0, r, :], gather_pair(k_refs, pp),
                            (((1,), (1,)), ((), ())), preferred_element_type=F32) * scale
        s_scr[r, :] = jnp.where(same_head, s + ccol_scr[r, :] + d_scr[pp:pp + 1, :], NEG_BIG)
    softmax_pv(width, lambda pp, p2: jnp.dot(p2, gather_pair(v_refs, pp), preferred_element_type=F32))

    @pl.when(g == pl.num_programs(1) - 1)
    def _():
        o_ref[0] = (acc_scr[...] / l_scr[...]).astype(o_ref.dtype)


def _attn_sample(q2, k_new, v_new, lf_new, lf_new_t, cache_k, cache_v, cache_lf2, page_table, n_heads):
    db, hr, _ = q2.shape
    t_len = hr // n_heads
    n_pairs = n_heads // 2
    w = n_heads * HEAD_DIM
    n_pages = page_table.shape[1]
    page = cache_lf2.shape[2] // 2
    assert page == LANES and t_len == 8 and n_heads % 2 == 0
    npg = _tile(n_pages, 8, 1)

    def page_map(b, g, pt, *, pg):
        return (pt[b, n_pages - 1 - (g * npg + pg)], 0, 0)

    def seq_map(b, g, pt):
        return (b, 0, 0)

    in_specs = [pl.BlockSpec((1, hr, HEAD_DIM), seq_map),
                pl.BlockSpec((1, page, w), seq_map),
                pl.BlockSpec((1, page, w), seq_map),
                pl.BlockSpec((1, t_len, LANES), seq_map),
                pl.BlockSpec((1, n_heads, LANES), seq_map)]
    in_specs += [pl.BlockSpec((1, page * n_heads, HEAD_DIM), functools.partial(page_map, pg=pg)) for pg in range(npg)]
    in_specs += [pl.BlockSpec((1, page * n_heads, HEAD_DIM), functools.partial(page_map, pg=pg)) for pg in range(npg)]
    in_specs += [pl.BlockSpec((1, n_pairs, 2 * page), functools.partial(page_map, pg=pg)) for pg in range(npg)]
    kern = functools.partial(_attn_sample_kernel, n_heads=n_heads, npg=npg, page=page,
                             t_len=t_len, scale=HEAD_DIM ** -0.5)
    return pl.pallas_call(
        kern,
        grid_spec=pltpu.PrefetchScalarGridSpec(
            num_scalar_prefetch=1,
            grid=(db, n_pages // npg),
            in_specs=in_specs,
            out_specs=pl.BlockSpec((1, hr, HEAD_DIM), seq_map),
            scratch_shapes=[pltpu.VMEM((hr, 1), F32),
                            pltpu.VMEM((hr, 1), F32),
                            pltpu.VMEM((hr, HEAD_DIM), F32),
                            pltpu.VMEM((hr, npg * 2 * page), F32),
                            pltpu.VMEM((n_pairs, npg * 2 * page), F32),
                            pltpu.VMEM((n_pairs, LANES), F32),
                            pltpu.VMEM((hr, 1), F32)]),
        out_shape=jax.ShapeDtypeStruct(q2.shape, BF16),
        compiler_params=_params(("parallel", "arbitrary")),
    )(page_table, q2, k_new, v_new, lf_new, lf_new_t,
      *([cache_k] * npg), *([cache_v] * npg), *([cache_lf2] * npg))


def _s5_kernel(*refs, n_real, nb, tc, half, nsub, exact, unroll, n_u):
    u_refs = refs[:n_u]
    bb_ref, cc_ref, a_ref, d_ref, init_ref, yg_ref, fin_ref, lhs_scr, x_scr, y_scr, st_scr = refs[n_u:]
    tci = pl.program_id(1)

    def u_block(b, s):
        cols = slice(s * LANES, (s + 1) * LANES)
        if n_u == 1:
            return u_refs[0][b * tc:(b + 1) * tc, cols]
        return u_refs[b][:, cols]

    @pl.when(tci == 0)
    def _():
        st_scr[...] = init_ref[...]
        lhs_scr[...] = jnp.zeros_like(lhs_scr)

    def dot(a, b):
        if exact:
            return jnp.dot(a, b, precision=lax.Precision.HIGHEST, preferred_element_type=F32)
        return jnp.dot(a, b, preferred_element_type=F32)

    for s in range(nsub):
        for b in range(n_real):
            lhs_scr[s, pl.ds(b, tc, stride=nb), :] = u_block(b, s)
        x_scr[s] = dot(lhs_scr[s], bb_ref[s])
    a_re = [a_ref[s, :, :half] for s in range(nsub)]
    a_im = [a_ref[s, :, half:] for s in range(nsub)]

    def step(t, carry):
        rows = pl.ds(pl.multiple_of(t * nb, nb), nb)
        out = []
        for s in range(nsub):
            xr, xi = carry[2 * s], carry[2 * s + 1]
            nr = a_re[s] * xr - a_im[s] * xi + x_scr[s, rows, :half]
            ni = a_re[s] * xi + a_im[s] * xr + x_scr[s, rows, half:]
            x_scr[s, rows, :half] = nr
            x_scr[s, rows, half:] = ni
            out += [nr, ni]
        return tuple(out)

    init = []
    for s in range(nsub):
        init += [st_scr[s, :, :half], st_scr[s, :, half:]]
    fin = lax.fori_loop(0, tc, step, tuple(init), unroll=unroll)
    for s in range(nsub):
        st_scr[s, :, :half] = fin[2 * s]
        st_scr[s, :, half:] = fin[2 * s + 1]
        y = dot(x_scr[s], cc_ref[s]) + d_ref[:, s * LANES:(s + 1) * LANES] * lhs_scr[s]
        y_scr[s] = jax.nn.gelu(y, approximate=True)
        for b in range(n_real):
            yg_ref[b, :, s * LANES:(s + 1) * LANES] = y_scr[s, pl.ds(b, tc, stride=nb), :]

    @pl.when(tci == pl.num_programs(1) - 1)
    def _():
        fin_ref[...] = st_scr[...]


def _s5(u, row_off, n_real, t_len, bb, cc, abar, d, init, *, exact):
    w = u.shape[1]
    nj, nb, two_half = init.shape
    half = two_half // 2
    nsub = _tile(nj, max(1, 16 // nb), 1)
    cw = nsub * LANES
    tc = _tile(t_len, max(8, 1024 // nb), 8)
    nt = t_len // tc
    if nt == 1 and row_off % (n_real * t_len) == 0:
        rb = row_off // (n_real * t_len)
        u_specs = [pl.BlockSpec((n_real * t_len, cw), lambda j, t: (rb, j))]
    else:
        assert row_off % tc == 0
        u_specs = [pl.BlockSpec((tc, cw), functools.partial(lambda j, t, o: (o + t, j), o=(row_off + b * t_len) // tc))
                   for b in range(n_real)]
    kern = functools.partial(_s5_kernel, n_real=n_real, nb=nb, tc=tc, half=half, nsub=nsub, exact=exact,
                             unroll=min(8, tc), n_u=len(u_specs))
    return pl.pallas_call(
        kern,
        grid=(nj // nsub, nt),
        in_specs=u_specs + [
                  pl.BlockSpec((nsub, LANES, two_half), lambda j, t: (j, 0, 0)),
                  pl.BlockSpec((nsub, two_half, LANES), lambda j, t: (j, 0, 0)),
                  pl.BlockSpec((nsub, 1, two_half), lambda j, t: (j, 0, 0)),
                  pl.BlockSpec((1, cw), lambda j, t: (0, j)),
                  pl.BlockSpec((nsub, nb, two_half), lambda j, t: (j, 0, 0))],
        out_specs=[pl.BlockSpec((n_real, tc, cw), lambda j, t: (0, t, j)),
                   pl.BlockSpec((nsub, nb, two_half), lambda j, t: (j, 0, 0))],
        out_shape=[jax.ShapeDtypeStruct((n_real, t_len, w), F32),
                   jax.ShapeDtypeStruct((nj, nb, two_half), F32)],
        scratch_shapes=[pltpu.VMEM((nsub, tc * nb, LANES), F32),
                        pltpu.VMEM((nsub, tc * nb, two_half), F32),
                        pltpu.VMEM((nsub, tc * nb, LANES), F32),
                        pltpu.VMEM((nsub, nb, two_half), F32)],
        compiler_params=_params(("parallel", "arbitrary")),
    )(*([u] * len(u_specs)), bb, cc, abar, d.reshape(1, w), init)


def _s5_params(a_re, a_im, log_dt, b_re, b_im, c_re, c_im):
    g, p = a_re.shape
    ch = b_re.shape[2]
    gpb = LANES // ch
    nj = g // gpb
    dt = jnp.exp(log_dt)[:, None]
    mag = jnp.exp(dt * a_re)
    ang = dt * a_im
    abar_re = mag * jnp.cos(ang)
    abar_im = mag * jnp.sin(ang)
    den = a_re * a_re + a_im * a_im
    nr = abar_re - 1.0
    coef_re = (nr * a_re + abar_im * a_im) / den
    coef_im = (abar_im * a_re - nr * a_im) / den
    bb_re = coef_re[..., None] * b_re - coef_im[..., None] * b_im
    bb_im = coef_re[..., None] * b_im + coef_im[..., None] * b_re
    eye = jnp.eye(gpb, dtype=F32)

    def pack_in(m):
        m = m.reshape(nj, gpb, p, ch).transpose(0, 1, 3, 2)
        return jnp.einsum('jgcp,gh->jgchp', m, eye).reshape(nj, gpb * ch, gpb * p)

    def pack_out(m):
        m = m.reshape(nj, gpb, ch, p).transpose(0, 1, 3, 2)
        return jnp.einsum('jgpc,gh->jgphc', m, eye).reshape(nj, gpb * p, gpb * ch)

    bb = jnp.concatenate([pack_in(bb_re), pack_in(bb_im)], axis=2)
    cc = jnp.concatenate([pack_out(c_re), pack_out(-c_im)], axis=1)
    abar = jnp.concatenate([abar_re.reshape(nj, 1, gpb * p), abar_im.reshape(nj, 1, gpb * p)], axis=2)
    return bb, cc, abar


def _pack_state(s_re, s_im, nj):
    nb = s_re.shape[0]
    return jnp.concatenate([s_re.reshape(nb, nj, -1).transpose(1, 0, 2),
                            s_im.reshape(nb, nj, -1).transpose(1, 0, 2)], axis=2)


def _unpack_state(fin, n_real, g, p):
    nj, nb, two_half = fin.shape
    half = two_half // 2
    re = fin[:, :n_real, :half].transpose(1, 0, 2).reshape(n_real, g, p)
    im = fin[:, :n_real, half:].transpose(1, 0, 2).reshape(n_real, g, p)
    return re, im


def kernel(x_prompt, x_sample, cache_k, cache_v, cache_logf, state_ssm_re, state_ssm_im, page_table, p_prompt, p_sample, ffn1_norm, ffn1_w_gate, ffn1_w_up, ffn1_w_down, mix_norm, w_in, b_f, q_norm, k_norm, ssm_a_re, ssm_a_im, ssm_log_dt, ssm_b_re, ssm_b_im, ssm_c_re, ssm_c_im, ssm_d, ssm_w_glu, w_branch_a, w_branch_b, w_out, ffn2_norm, ffn2_w_gate, ffn2_w_up, ffn2_w_down, ple_norm, w_ple, w_ple_gate):
    depth = ffn1_norm.shape[0]
    assert depth == 1
    bsz, seq, dm = x_prompt.shape
    dbsz, dseq, _ = x_sample.shape
    n_heads = cache_k.shape[3]
    attn_w = n_heads * HEAD_DIM
    n_groups, ssm_p = ssm_a_re.shape[1], ssm_a_re.shape[2]
    ssm_w = n_groups * SSM_CH
    n_phys, page = cache_k.shape[1], cache_k.shape[2]
    mp = bsz * seq
    ms = dbsz * dseq
    o3 = 3 * attn_w
    o4 = o3 + n_heads

    pe_in = jnp.concatenate([p_prompt[0].reshape(mp, -1), p_sample[0].reshape(ms, -1)], axis=0).astype(BF16)

    x1, h2 = _ffn([x_prompt.reshape(mp, dm), x_sample.reshape(ms, dm)], ffn1_norm[0],
                  ffn1_w_gate[0], ffn1_w_up[0], ffn1_w_down[0], mix_norm[0])

    w_in_t = jnp.swapaxes(w_in[0], 0, 1)
    w_qkv_t = w_in_t[:o3].astype(BF16)
    w_rest_t = w_in_t[o4:].astype(BF16)
    w_f_t = jnp.pad(w_in_t[o3:o4], ((0, LANES - n_heads), (0, 0)))
    b_f_pad = jnp.pad(b_f[0], (0, LANES - n_heads)).reshape(1, LANES)
    qg = jnp.tile(q_norm[0], n_heads).reshape(1, attn_w)
    kg = jnp.tile(k_norm[0], n_heads).reshape(1, attn_w)
    (q,) = _mm([h2], [(w_qkv_t, 0, 0, True)], lambda a, r, c: [_head_rms(a[0], c[0])], [BF16], attn_w, vecs=[qg])
    kb_p, k_out_p = _kv_proj(h2, w_qkv_t, attn_w, kg, n_heads, normalise=True, row_off=0, m=mp)
    vb_p, v_out_p = _kv_proj(h2, w_qkv_t, 2 * attn_w, kg, n_heads, normalise=False, row_off=0, m=mp)
    kb_s, k_out_s = _kv_proj(h2, w_qkv_t, attn_w, kg, n_heads, normalise=True, row_off=mp, m=ms)
    vb_s, v_out_s = _kv_proj(h2, w_qkv_t, 2 * attn_w, kg, n_heads, normalise=False, row_off=mp, m=ms)
    (logf,) = _mm([h2], [(w_f_t, 0, 0, True)], lambda a, r, c: [_log_sigmoid(a[0] + c[0])], [F32], LANES,
                  vecs=[b_f_pad])
    (u,) = _mm([h2], [(w_rest_t, 0, 0, True)], lambda a, r, c: [a[0]], [F32], ssm_w)
    gate_a, gate_b = _mm([h2], [(w_rest_t, 0, ssm_w, True), (w_rest_t, 0, ssm_w + dm, True)],
                         lambda a, r, c: [jax.nn.sigmoid(a[0]), jax.nn.sigmoid(a[1])], [F32, F32], dm)

    lf_p = logf[:mp].reshape(bsz, seq, LANES)
    c_p = _cumsum_seq(lf_p)
    c_p_t = c_p[:, :, :n_heads].transpose(0, 2, 1)
    a_p = _attn_prompt(q, kb_p, vb_p, c_p, c_p_t, n_heads)

    n_pairs = n_heads // 2
    lf_s = logf[mp:].reshape(dbsz, dseq, LANES)
    lf_s_t = jnp.pad(lf_s[:, :, :n_heads].transpose(0, 2, 1), ((0, 0), (0, 0), (0, LANES - dseq)))
    pad_new = ((0, 0), (0, page - dseq), (0, 0))
    k_new = jnp.pad(kb_s.reshape(dbsz, dseq, attn_w), pad_new)
    v_new = jnp.pad(vb_s.reshape(dbsz, dseq, attn_w), pad_new)
    q2 = q[mp:].reshape(dbsz, dseq, 2, n_pairs, HEAD_DIM).transpose(0, 3, 2, 1, 4).reshape(dbsz, n_heads * dseq, HEAD_DIM)
    cache_lf2 = cache_logf[0].reshape(n_phys, page, 2, n_pairs).transpose(0, 3, 1, 2).reshape(n_phys, n_pairs, 2 * page)
    a_s2 = _attn_sample(q2, k_new, v_new, lf_s, lf_s_t,
                        cache_k[0].reshape(n_phys, page * n_heads, HEAD_DIM),
                        cache_v[0].reshape(n_phys, page * n_heads, HEAD_DIM),
                        cache_lf2, page_table, n_heads)
    a_s = a_s2.reshape(dbsz, n_pairs, 2, dseq, HEAD_DIM).transpose(0, 3, 2, 1, 4).reshape(ms, attn_w)
    a = jnp.concatenate([a_p, a_s], axis=0)

    bb, cc, abar = _s5_params(ssm_a_re[0], ssm_a_im[0], ssm_log_dt[0], ssm_b_re[0], ssm_b_im[0],
                              ssm_c_re[0], ssm_c_im[0])
    nj = bb.shape[0]
    d_vec = ssm_d[0].reshape(ssm_w)
    nb_p = -(-bsz // 8) * 8
    init_p = jnp.zeros((nj, nb_p, bb.shape[2]), F32)
    yg_p, fin_p = _s5(u, 0, bsz, seq, bb, cc, abar, d_vec, init_p, exact=False)
    nb_s = -(-dbsz // 8) * 8
    pad_state = ((0, nb_s - dbsz), (0, 0), (0, 0))
    init_s = _pack_state(jnp.pad(state_ssm_re[0], pad_state), jnp.pad(state_ssm_im[0], pad_state), nj)
    yg_s, fin_s = _s5(u, mp, dbsz, dseq, bb, cc, abar, d_vec, init_s, exact=True)
    yg = jnp.concatenate([yg_p.reshape(mp, ssm_w), yg_s.reshape(ms, ssm_w)], axis=0)
    (b_out,) = _mm([yg], [(ssm_w_glu[0], 0, 0, False)], lambda a, r, c: [r[0] * jax.nn.sigmoid(a[0])], [BF16],
                   ssm_w, rows=[(yg, 0)], tm_target=528)

    (merged,) = _mm([a, b_out], [(w_branch_a[0], 0, 0, False), (w_branch_b[0], 1, 0, False)],
                    lambda acc, r, c: [r[0] * acc[0] + r[1] * acc[1]], [BF16], dm,
                    rows=[(gate_a, 0), (gate_b, 0)])
    (x2,) = _mm([merged], [(w_out[0], 0, 0, False)], lambda acc, r, c: [r[0] + acc[0]], [F32], dm, rows=[(x1, 0)])

    x3, h4 = _ffn([x2], ffn2_norm[0], ffn2_w_gate[0], ffn2_w_up[0], ffn2_w_down[0], ple_norm[0])

    def ple(row_off, rows):
        (y,) = _mm([pe_in, h4], [(w_ple[0], 0, 0, False), (w_ple_gate[0], 1, 0, False)],
                   lambda acc, r, c: [r[0] + acc[0] * jax.nn.sigmoid(acc[1])], [F32], dm, rows=[(x3, 0)],
                   row_off=row_off, m=rows)
        return y

    y_p = ple(0, mp)
    y_s = ple(mp, ms)

    sre_p, sim_p = _unpack_state(fin_p, bsz, n_groups, ssm_p)
    sre_s, sim_s = _unpack_state(fin_s, dbsz, n_groups, ssm_p)
    return (y_p.reshape(bsz, seq, dm), y_s.reshape(dbsz, dseq, dm),
            k_out_p.reshape(1, bsz, seq, n_heads, HEAD_DIM), v_out_p.reshape(1, bsz, seq, n_heads, HEAD_DIM),
            lf_p[:, :, :n_heads].reshape(1, bsz, seq, n_heads),
            sre_p[None], sim_p[None],
            k_out_s.reshape(1, dbsz, dseq, n_heads, HEAD_DIM), v_out_s.reshape(1, dbsz, dseq, n_heads, HEAD_DIM),
            lf_s[:, :, :n_heads].reshape(1, dbsz, dseq, n_heads),
            sre_s[None], sim_s[None])
```

```python
import functools
import math

import jax
import jax.numpy as jnp
from jax import lax
from jax.experimental import pallas as pl
from jax.experimental.pallas import tpu as pltpu

F32 = jnp.float32
BF16 = jnp.bfloat16

EPS = 1e-6
HEAD_DIM = 128
SSM_CH = 16
LANES = 128
VMEM_LIMIT_BYTES = 56 * 1024 * 1024
FFN_VMEM_LIMIT_BYTES = 60 * 1024 * 1024
NEG_BIG = -1e30


def _params(sem):
    return pltpu.CompilerParams(dimension_semantics=sem, vmem_limit_bytes=VMEM_LIMIT_BYTES)


def _tile(n, target, mult):
    best = None
    for t in range(mult, min(n, target) + 1, mult):
        if n % t == 0:
            best = t
    assert best is not None, (n, target, mult)
    return best


def _ffn_kernel(*refs, tm, rc, rows_a, two_src):
    if two_src:
        xa_hbm, xb_hbm = refs[:2]
        refs = refs[2:]
    else:
        xa_hbm = refs[0]
        refs = refs[1:]
    g_ref, g2_ref, wg_ref, wu_ref, wd_ref, o_hbm, h_hbm, acc, h, sem = refs
    i = pl.program_id(0)
    k = pl.program_id(1)
    n_full = rows_a // tm
    part = rows_a - n_full * tm

    def norm_rows(gain_ref):
        def body(r, carry):
            rows = pl.ds(pl.multiple_of(r * rc, rc), rc)
            xs = acc[rows, :]
            y = xs * lax.rsqrt(jnp.mean(xs * xs, axis=-1, keepdims=True) + EPS)
            h[rows, :] = (y * gain_ref[...]).astype(BF16)
            return carry
        lax.fori_loop(0, tm // rc, body, 0)

    @pl.when(k == 0)
    def _():
        @pl.when(i < n_full)
        def _():
            cp = pltpu.make_async_copy(xa_hbm.at[pl.ds(i * tm, tm)], acc, sem.at[0])
            cp.start()
            cp.wait()

        if two_src:
            @pl.when(i == n_full)
            def _():
                cps = [pltpu.make_async_copy(xb_hbm.at[pl.ds(0, tm - part)], acc.at[pl.ds(part, tm - part)], sem.at[1])]
                if part:
                    cps.append(pltpu.make_async_copy(xa_hbm.at[pl.ds(n_full * tm, part)], acc.at[pl.ds(0, part)],
                                                     sem.at[0]))
                for cp in cps:
                    cp.start()
                for cp in cps:
                    cp.wait()

        norm_rows(g_ref)

    hb = h[...]
    g = jnp.dot(hb, wg_ref[...].astype(BF16), preferred_element_type=F32)
    u = jnp.dot(hb, wu_ref[...].astype(BF16), preferred_element_type=F32)
    a = (0.5 * (g * jax.nn.sigmoid(g)) * u).astype(BF16)
    acc[...] += jnp.dot(a, wd_ref[...].astype(BF16), preferred_element_type=F32)

    @pl.when(k == pl.num_programs(1) - 1)
    def _():
        cp_x = pltpu.make_async_copy(acc, o_hbm.at[pl.ds(i * tm, tm)], sem.at[0])
        cp_x.start()
        norm_rows(g2_ref)
        cp_h = pltpu.make_async_copy(h, h_hbm.at[pl.ds(i * tm, tm)], sem.at[1])
        cp_h.start()
        cp_x.wait()
        cp_h.wait()


def _ffn(xs, gain, w_gate, w_up, w_down, next_gain):
    d = xs[0].shape[1]
    m = sum(x.shape[0] for x in xs)
    dff = w_gate.shape[1]
    tm = _tile(m, 1056, 16)
    tf = _tile(dff, 256, LANES)
    rc = _tile(tm, 64, 8)
    rows_a = xs[0].shape[0]
    two_src = len(xs) == 2
    if two_src:
        assert (rows_a // tm + 1) * tm == m
    else:
        assert rows_a == m
    return pl.pallas_call(
        functools.partial(_ffn_kernel, tm=tm, rc=rc, rows_a=rows_a, two_src=two_src),
        grid=(m // tm, dff // tf),
        in_specs=[pl.BlockSpec(memory_space=pl.ANY)] * len(xs) + [
                  pl.BlockSpec((1, d), lambda i, k: (0, 0)),
                  pl.BlockSpec((1, d), lambda i, k: (0, 0)),
                  pl.BlockSpec((d, tf), lambda i, k: (0, k)),
                  pl.BlockSpec((d, tf), lambda i, k: (0, k)),
                  pl.BlockSpec((tf, d), lambda i, k: (k, 0))],
        out_specs=[pl.BlockSpec(memory_space=pl.ANY), pl.BlockSpec(memory_space=pl.ANY)],
        out_shape=[jax.ShapeDtypeStruct((m, d), F32), jax.ShapeDtypeStruct((m, d), BF16)],
        scratch_shapes=[pltpu.VMEM((tm, d), F32),
                        pltpu.VMEM((tm, d), BF16),
                        pltpu.SemaphoreType.DMA((2,))],
        compiler_params=pltpu.CompilerParams(dimension_semantics=("arbitrary", "arbitrary"),
                                             vmem_limit_bytes=FFN_VMEM_LIMIT_BYTES),
    )(*xs, gain.reshape(1, d), next_gain.reshape(1, d), w_gate, w_up, w_down)


def _mxu_dot(lhs, w, transposed=False):
    if lhs.dtype == BF16 and w.dtype != BF16:
        w = w.astype(BF16)
    if transposed:
        return lax.dot_general(lhs, w, (((1,), (1,)), ((), ())), preferred_element_type=F32)
    return jnp.dot(lhs, w, preferred_element_type=F32)


def _mm_kernel(*refs, n_lhs, dots, n_rows, n_vecs, epilogue):
    lhs_refs = refs[:n_lhs]
    w_refs = refs[n_lhs:n_lhs + len(dots)]
    p = n_lhs + len(dots)
    row_refs = refs[p:p + n_rows]
    vec_refs = refs[p + n_rows:p + n_rows + n_vecs]
    out_refs = refs[p + n_rows + n_vecs:]
    lhs_vals = [r[...] for r in lhs_refs]
    accs = [_mxu_dot(lhs_vals[li], w_ref[...], tr) for (li, tr), w_ref in zip(dots, w_refs)]
    outs = epilogue(accs, [r[...] for r in row_refs], [v[...] for v in vec_refs])
    for o_ref, o in zip(out_refs, outs):
        o_ref[...] = o.astype(o_ref.dtype)


def _mm(lhs, ws, epilogue, out_dtypes, n, *, rows=(), vecs=(), tm_target=1056, tn_target=512, row_off=0, m=None):
    if m is None:
        m = lhs[0].shape[0]
    tm = _tile(math.gcd(m, row_off), tm_target, 16)
    tn = _tile(math.gcd(n, *[off for _, _, off, _ in ws], *[off for _, off in rows]), tn_target, LANES)
    ro = row_off // tm
    in_specs = [pl.BlockSpec((tm, a.shape[1]), lambda i, j: (i + ro, 0)) for a in lhs]
    for w, _, off, tr in ws:
        if tr:
            in_specs.append(pl.BlockSpec((tn, w.shape[1]), functools.partial(lambda i, j, o: (j + o, 0), o=off // tn)))
        else:
            in_specs.append(pl.BlockSpec((w.shape[0], tn), functools.partial(lambda i, j, o: (0, j + o), o=off // tn)))
    in_specs += [pl.BlockSpec((tm, tn), functools.partial(lambda i, j, o: (i + ro, j + o), o=off // tn))
                 for _, off in rows]
    in_specs += [pl.BlockSpec((1, tn), lambda i, j: (0, j)) for _ in vecs]
    out_specs = [pl.BlockSpec((tm, tn), lambda i, j: (i, j)) for _ in out_dtypes]
    out_shape = [jax.ShapeDtypeStruct((m, n), dt) for dt in out_dtypes]
    kern = functools.partial(_mm_kernel, n_lhs=len(lhs), dots=tuple((li, tr) for _, li, _, tr in ws),
                             n_rows=len(rows), n_vecs=len(vecs), epilogue=epilogue)
    outs = pl.pallas_call(
        kern,
        grid=(m // tm, n // tn),
        in_specs=in_specs,
        out_specs=out_specs,
        out_shape=out_shape,
        compiler_params=_params(("parallel", "arbitrary")),
    )(*lhs, *[w for w, _, _, _ in ws], *[r for r, _ in rows], *vecs)
    return outs


def _head_rms(z, gain):
    parts = []
    for c in range(z.shape[1] // HEAD_DIM):
        zc = z[:, c * HEAD_DIM:(c + 1) * HEAD_DIM]
        y = zc * lax.rsqrt(jnp.mean(zc * zc, axis=-1, keepdims=True) + EPS)
        parts.append(y * gain[:, c * HEAD_DIM:(c + 1) * HEAD_DIM])
    return jnp.concatenate(parts, axis=1) if len(parts) > 1 else parts[0]


def _log_sigmoid(x):
    return jnp.minimum(x, 0.0) - jnp.log1p(jnp.exp(-jnp.abs(x)))


def _kv_proj_kernel(h_ref, w_ref, g_ref, ob_ref, of_ref, *, normalise):
    z = _mxu_dot(h_ref[...], w_ref[...], transposed=True)
    if normalise:
        z = _head_rms(z, g_ref[...])
    ob_ref[...] = z.astype(ob_ref.dtype)
    for hh in range(of_ref.shape[1]):
        of_ref[:, hh, :] = z[:, hh * HEAD_DIM:(hh + 1) * HEAD_DIM]


def _kv_proj(h, w_t, col_off, gain_tiled, n_heads, *, normalise, row_off, m):
    kdim = h.shape[1]
    hb = min(n_heads, 8)
    tn = hb * HEAD_DIM
    tm = _tile(math.gcd(m, row_off), 1056, 16)
    assert col_off % tn == 0
    off = col_off // tn
    ro = row_off // tm
    return pl.pallas_call(
        functools.partial(_kv_proj_kernel, normalise=normalise),
        grid=(m // tm, n_heads // hb),
        in_specs=[pl.BlockSpec((tm, kdim), lambda i, j: (i + ro, 0)),
                  pl.BlockSpec((tn, kdim), lambda i, j: (j + off, 0)),
                  pl.BlockSpec((1, tn), lambda i, j: (0, j))],
        out_specs=[pl.BlockSpec((tm, tn), lambda i, j: (i, j)),
                   pl.BlockSpec((tm, hb, HEAD_DIM), lambda i, j: (i, j, 0))],
        out_shape=[jax.ShapeDtypeStruct((m, n_heads * HEAD_DIM), BF16),
                   jax.ShapeDtypeStruct((m, n_heads, HEAD_DIM), F32)],
        compiler_params=_params(("parallel", "arbitrary")),
    )(h, w_t, gain_tiled)


def _cumsum_kernel(x_ref, o_ref):
    x = x_ref[0]
    s = x.shape[0]
    row = lax.broadcasted_iota(jnp.int32, x.shape, 0)
    k = 1
    while k < s:
        x = x + jnp.where(row >= k, pltpu.roll(x, k, 0), 0.0)
        k *= 2
    o_ref[0] = x


def _cumsum_seq(x):
    b, s, w = x.shape
    return pl.pallas_call(
        _cumsum_kernel,
        grid=(b,),
        in_specs=[pl.BlockSpec((1, s, w), lambda i: (i, 0, 0))],
        out_specs=pl.BlockSpec((1, s, w), lambda i: (i, 0, 0)),
        out_shape=jax.ShapeDtypeStruct((b, s, w), F32),
        compiler_params=_params(("parallel",)),
    )(x)


def _attn_prompt_kernel(q_ref, k_ref, v_ref, c_ref, ct_ref, o_ref, m_scr, l_scr, acc_scr, *, tq, tk, hps, scale):
    hg = pl.program_id(1)
    qi = pl.program_id(2)
    m_scr[...] = jnp.full_like(m_scr, NEG_BIG)
    l_scr[...] = jnp.zeros_like(l_scr)
    acc_scr[...] = jnp.zeros_like(acc_scr)
    c_blk = c_ref[0]
    lane = lax.broadcasted_iota(jnp.int32, c_blk.shape, 1)
    c_cols = [jnp.sum(jnp.where(lane == hg * hps + hh, c_blk, 0.0), axis=1, keepdims=True) for hh in range(hps)]
    q_pos = qi * tq + lax.broadcasted_iota(jnp.int32, (tq, tk), 0)
    k_off = lax.broadcasted_iota(jnp.int32, (tq, tk), 1)

    def step(j, carry):
        start = pl.multiple_of(j * tk, tk)
        causal = j * tk + k_off <= q_pos
        for hh in range(hps):
            cols = slice(hh * HEAD_DIM, (hh + 1) * HEAD_DIM)
            kb = k_ref[pl.ds(start, tk), cols]
            vb = v_ref[pl.ds(start, tk), cols]
            c_row = ct_ref[0, pl.ds(hg * hps + hh, 1), pl.ds(start, tk)]
            s = lax.dot_general(q_ref[:, cols], kb, (((1,), (1,)), ((), ())),
                                preferred_element_type=F32) * scale
            s = jnp.where(causal, s + c_cols[hh] - c_row, NEG_BIG)
            m_old = m_scr[hh]
            m_new = jnp.maximum(m_old, jnp.max(s, axis=1, keepdims=True))
            p = jnp.exp(s - m_new)
            alpha = jnp.exp(m_old - m_new)
            l_scr[hh] = alpha * l_scr[hh] + jnp.sum(p, axis=1, keepdims=True)
            acc_scr[hh] = alpha * acc_scr[hh] + jnp.dot(p.astype(BF16), vb, preferred_element_type=F32)
            m_scr[hh] = m_new
        return carry

    lax.fori_loop(0, (qi + 1) * (tq // tk), step, 0)
    for hh in range(hps):
        o_ref[:, hh * HEAD_DIM:(hh + 1) * HEAD_DIM] = (acc_scr[hh] / l_scr[hh]).astype(o_ref.dtype)


def _attn_prompt(q, k, v, c, c_t, n_heads):
    b, s, _ = c.shape
    tq = _tile(s, 1024, LANES)
    tk = tq
    hps = _tile(n_heads, 2, 1)
    w = hps * HEAD_DIM
    nq = s // tq
    return pl.pallas_call(
        functools.partial(_attn_prompt_kernel, tq=tq, tk=tk, hps=hps, scale=HEAD_DIM ** -0.5),
        grid=(b, n_heads // hps, nq),
        in_specs=[pl.BlockSpec((tq, w), lambda bi, h, qi: (bi * nq + qi, h)),
                  pl.BlockSpec((s, w), lambda bi, h, qi: (bi, h)),
                  pl.BlockSpec((s, w), lambda bi, h, qi: (bi, h)),
                  pl.BlockSpec((1, tq, LANES), lambda bi, h, qi: (bi, qi, 0)),
                  pl.BlockSpec((1, n_heads, s), lambda bi, h, qi: (bi, 0, 0))],
        out_specs=pl.BlockSpec((tq, w), lambda bi, h, qi: (bi * nq + qi, h)),
        out_shape=jax.ShapeDtypeStruct((b * s, q.shape[1]), BF16),
        scratch_shapes=[pltpu.VMEM((hps, tq, 1), F32),
                        pltpu.VMEM((hps, tq, 1), F32),
                        pltpu.VMEM((hps, tq, HEAD_DIM), F32)],
        compiler_params=_params(("parallel", "parallel", "arbitrary")),
    )(q, k, v, c, c_t)


def _attn_sample_kernel(pt_ref, q_ref, kn_ref, vn_ref, lfn_ref, lfnt_ref, *rest,
                        n_heads, npg, page, t_len, scale):
    k_refs = rest[:npg]
    v_refs = rest[npg:2 * npg]
    lf_refs = rest[2 * npg:3 * npg]
    o_ref = rest[3 * npg]
    m_scr, l_scr, acc_scr, s_scr, d_scr, carry_scr, ccol_scr = rest[3 * npg + 1:]
    g = pl.program_id(1)
    n_pairs = n_heads // 2
    pr = 2 * t_len
    w2 = 2 * page
    lane_p = lax.broadcasted_iota(jnp.int32, (n_pairs, LANES), 1)
    even_p = (lane_p & 1) == 0
    row_half = lax.broadcasted_iota(jnp.int32, (pr, LANES), 0) // t_len

    def pair_rows(p):
        return slice(p * pr, (p + 1) * pr)

    def head_cols(hh):
        return slice(hh * HEAD_DIM, (hh + 1) * HEAD_DIM)

    def softmax_pv(width, pv_of_pair):
        s = s_scr[:, :width]
        m_old = m_scr[...]
        m_new = jnp.maximum(m_old, jnp.max(s, axis=1, keepdims=True))
        alpha = jnp.exp(m_old - m_new)
        p = jnp.exp(s - m_new)
        l_scr[...] = alpha * l_scr[...] + jnp.sum(p, axis=1, keepdims=True)
        m_scr[...] = m_new
        s_scr[:, :width] = p
        for pp in range(n_pairs):
            r = pair_rows(pp)
            acc_scr[r, :] = alpha[r] * acc_scr[r, :] + pv_of_pair(pp, s_scr[r, :width].astype(BF16))

    @pl.when(g == 0)
    def _():
        sub = lax.broadcasted_iota(jnp.int32, (t_len, LANES), 0)
        lane_t = lax.broadcasted_iota(jnp.int32, (t_len, LANES), 1)
        lane_h = lax.broadcasted_iota(jnp.int32, (n_heads, LANES), 1)
        cn = lfn_ref[0]
        k = 1
        while k < t_len:
            cn = cn + jnp.where(sub >= k, pltpu.roll(cn, k, 0), 0.0)
            k *= 2
        cnt = lfnt_ref[0]
        k = 1
        while k < t_len:
            cnt = cnt + jnp.where(lane_h >= k, pltpu.roll(cnt, k, 1), 0.0)
            k *= 2
        carry_scr[...] = jnp.zeros_like(carry_scr)
        m_scr[...] = jnp.full_like(m_scr, NEG_BIG)
        l_scr[...] = jnp.zeros_like(l_scr)
        acc_scr[...] = jnp.zeros_like(acc_scr)
        t_pos = lax.broadcasted_iota(jnp.int32, (pr, page), 0) % t_len
        s_pos = lax.broadcasted_iota(jnp.int32, (pr, page), 1)
        for pp in range(n_pairs):
            r = pair_rows(pp)
            h0, h1 = pp, pp + n_pairs
            for half, hh in ((0, h0), (1, h1)):
                ccol_scr[pp * pr + half * t_len:pp * pr + (half + 1) * t_len, :] = jnp.sum(
                    jnp.where(lane_t == hh, cn, 0.0), axis=1, keepdims=True)
            q2 = q_ref[0, r, :]
            s0 = lax.dot_general(q2, kn_ref[0, :, head_cols(h0)], (((1,), (1,)), ((), ())),
                                 preferred_element_type=F32)
            s1 = lax.dot_general(q2, kn_ref[0, :, head_cols(h1)], (((1,), (1,)), ((), ())),
                                 preferred_element_type=F32)
            c_row = jnp.where(row_half == 0, cnt[h0:h0 + 1, :], cnt[h1:h1 + 1, :])
            s = jnp.where(row_half == 0, s0, s1) * scale + ccol_scr[r, :] - c_row
            s_scr[r, :page] = jnp.where(s_pos <= t_pos, s, NEG_BIG)

        def pv_new(pp, p2):
            pv0 = jnp.dot(p2, vn_ref[0, :, head_cols(pp)], preferred_element_type=F32)
            pv1 = jnp.dot(p2, vn_ref[0, :, head_cols(pp + n_pairs)], preferred_element_type=F32)
            return jnp.where(row_half == 0, pv0, pv1)

        softmax_pv(page, pv_new)

    def sfx_par(x):
        k = 2
        while k < LANES:
            x = x + jnp.where(lane_p + k < LANES, pltpu.roll(x, LANES - k, 1), 0.0)
            k *= 2
        return x

    def tot_par(x):
        te = jnp.sum(jnp.where(even_p, x, 0.0), axis=1, keepdims=True)
        to = jnp.sum(jnp.where(even_p, 0.0, x), axis=1, keepdims=True)
        return jnp.where(even_p, te, to)

    for pg in range(npg):
        lo = lf_refs[pg][0, :, :LANES]
        hi = lf_refs[pg][0, :, LANES:]
        tot_hi = tot_par(hi)
        carry = carry_scr[...]
        d_scr[:, pg * w2:pg * w2 + LANES] = sfx_par(lo) + tot_hi - lo + carry
        d_scr[:, pg * w2 + LANES:(pg + 1) * w2] = sfx_par(hi) - hi + carry
        carry_scr[...] = carry + tot_par(lo) + tot_hi

    def gather_pair(refs, pp):
        return jnp.concatenate([r[0, pl.ds(pp, w2, stride=n_pairs), :] for r in refs], axis=0).astype(BF16)

    width = npg * w2
    col_half = lax.broadcasted_iota(jnp.int32, (pr, width), 1) & 1
    same_head = col_half == lax.broadcasted_iota(jnp.int32, (pr, width), 0) // t_len
    for pp in range(n_pairs):
        r = pair_rows(pp)
        s = lax.dot_general(q_ref[0, r, :], gather_pair(k_refs, pp),
                            (((1,), (1,)), ((), ())), preferred_element_type=F32) * scale
        s_scr[r, :] = jnp.where(same_head, s + ccol_scr[r, :] + d_scr[pp:pp + 1, :], NEG_BIG)
    softmax_pv(width, lambda pp, p2: jnp.dot(p2, gather_pair(v_refs, pp), preferred_element_type=F32))

    @pl.when(g == pl.num_programs(1) - 1)
    def _():
        o_ref[0] = (acc_scr[...] / l_scr[...]).astype(o_ref.dtype)


def _attn_sample(q2, k_new, v_new, lf_new, lf_new_t, cache_k, cache_v, cache_lf2, page_table, n_heads):
    db, hr, _ = q2.shape
    t_len = hr // n_heads
    n_pairs = n_heads // 2
    w = n_heads * HEAD_DIM
    n_pages = page_table.shape[1]
    page = cache_lf2.shape[2] // 2
    assert page == LANES and t_len == 8 and n_heads % 2 == 0
    npg = _tile(n_pages, 8, 1)

    def page_map(b, g, pt, *, pg):
        return (pt[b, n_pages - 1 - (g * npg + pg)], 0, 0)

    def seq_map(b, g, pt):
        return (b, 0, 0)

    in_specs = [pl.BlockSpec((1, hr, HEAD_DIM), seq_map),
                pl.BlockSpec((1, page, w), seq_map),
                pl.BlockSpec((1, page, w), seq_map),
                pl.BlockSpec((1, t_len, LANES), seq_map),
                pl.BlockSpec((1, n_heads, LANES), seq_map)]
    in_specs += [pl.BlockSpec((1, page * n_heads, HEAD_DIM), functools.partial(page_map, pg=pg)) for pg in range(npg)]
    in_specs += [pl.BlockSpec((1, page * n_heads, HEAD_DIM), functools.partial(page_map, pg=pg)) for pg in range(npg)]
    in_specs += [pl.BlockSpec((1, n_pairs, 2 * page), functools.partial(page_map, pg=pg)) for pg in range(npg)]
    kern = functools.partial(_attn_sample_kernel, n_heads=n_heads, npg=npg, page=page,
                             t_len=t_len, scale=HEAD_DIM ** -0.5)
    return pl.pallas_call(
        kern,
        grid_spec=pltpu.PrefetchScalarGridSpec(
            num_scalar_prefetch=1,
            grid=(db, n_pages // npg),
            in_specs=in_specs,
            out_specs=pl.BlockSpec((1, hr, HEAD_DIM), seq_map),
            scratch_shapes=[pltpu.VMEM((hr, 1), F32),
                            pltpu.VMEM((hr, 1), F32),
                            pltpu.VMEM((hr, HEAD_DIM), F32),
                            pltpu.VMEM((hr, npg * 2 * page), F32),
                            pltpu.VMEM((n_pairs, npg * 2 * page), F32),
                            pltpu.VMEM((n_pairs, LANES), F32),
                            pltpu.VMEM((hr, 1), F32)]),
        out_shape=jax.ShapeDtypeStruct(q2.shape, BF16),
        compiler_params=_params(("parallel", "arbitrary")),
    )(page_table, q2, k_new, v_new, lf_new, lf_new_t,
      *([cache_k] * npg), *([cache_v] * npg), *([cache_lf2] * npg))


def _s5_kernel(*refs, n_real, nb, tc, half, nsub, exact, unroll, n_u):
    u_refs = refs[:n_u]
    bb_ref, cc_ref, a_ref, d_ref, init_ref, yg_ref, fin_ref, lhs_scr, x_scr, y_scr, st_scr = refs[n_u:]
    tci = pl.program_id(1)

    def u_block(b, s):
        cols = slice(s * LANES, (s + 1) * LANES)
        if n_u == 1:
            return u_refs[0][b * tc:(b + 1) * tc, cols]
        return u_refs[b][:, cols]

    @pl.when(tci == 0)
    def _():
        st_scr[...] = init_ref[...]
        lhs_scr[...] = jnp.zeros_like(lhs_scr)

    def dot(a, b):
        if exact:
            return jnp.dot(a, b, precision=lax.Precision.HIGHEST, preferred_element_type=F32)
        return jnp.dot(a, b, preferred_element_type=F32)

    for s in range(nsub):
        for b in range(n_real):
            lhs_scr[s, pl.ds(b, tc, stride=nb), :] = u_block(b, s)
        x_scr[s] = dot(lhs_scr[s], bb_ref[s])
    a_re = [a_ref[s, :, :half] for s in range(nsub)]
    a_im = [a_ref[s, :, half:] for s in range(nsub)]

    def step(t, carry):
        rows = pl.ds(pl.multiple_of(t * nb, nb), nb)
        out = []
        for s in range(nsub):
            xr, xi = carry[2 * s], carry[2 * s + 1]
            nr = a_re[s] * xr - a_im[s] * xi + x_scr[s, rows, :half]
            ni = a_re[s] * xi + a_im[s] * xr + x_scr[s, rows, half:]
            x_scr[s, rows, :half] = nr
            x_scr[s, rows, half:] = ni
            out += [nr, ni]
        return tuple(out)

    init = []
    for s in range(nsub):
        init += [st_scr[s, :, :half], st_scr[s, :, half:]]
    fin = lax.fori_loop(0, tc, step, tuple(init), unroll=unroll)
    for s in range(nsub):
        st_scr[s, :, :half] = fin[2 * s]
        st_scr[s, :, half:] = fin[2 * s + 1]
        y = dot(x_scr[s], cc_ref[s]) + d_ref[:, s * LANES:(s + 1) * LANES] * lhs_scr[s]
        y_scr[s] = jax.nn.gelu(y, approximate=True)
        for b in range(n_real):
            yg_ref[b, :, s * LANES:(s + 1) * LANES] = y_scr[s, pl.ds(b, tc, stride=nb), :]

    @pl.when(tci == pl.num_programs(1) - 1)
    def _():
        fin_ref[...] = st_scr[...]


def _s5(u, row_off, n_real, t_len, bb, cc, abar, d, init, *, exact):
    w = u.shape[1]
    nj, nb, two_half = init.shape
    half = two_half // 2
    nsub = _tile(nj, max(1, 16 // nb), 1)
    cw = nsub * LANES
    tc = _tile(t_len, max(8, 1024 // nb), 8)
    nt = t_len // tc
    if nt == 1 and row_off % (n_real * t_len) == 0:
        rb = row_off // (n_real * t_len)
        u_specs = [pl.BlockSpec((n_real * t_len, cw), lambda j, t: (rb, j))]
    else:
        assert row_off % tc == 0
        u_specs = [pl.BlockSpec((tc, cw), functools.partial(lambda j, t, o: (o + t, j), o=(row_off + b * t_len) // tc))
                   for b in range(n_real)]
    kern = functools.partial(_s5_kernel, n_real=n_real, nb=nb, tc=tc, half=half, nsub=nsub, exact=exact,
                             unroll=min(8, tc), n_u=len(u_specs))
    return pl.pallas_call(
        kern,
        grid=(nj // nsub, nt),
        in_specs=u_specs + [
                  pl.BlockSpec((nsub, LANES, two_half), lambda j, t: (j, 0, 0)),
                  pl.BlockSpec((nsub, two_half, LANES), lambda j, t: (j, 0, 0)),
                  pl.BlockSpec((nsub, 1, two_half), lambda j, t: (j, 0, 0)),
                  pl.BlockSpec((1, cw), lambda j, t: (0, j)),
                  pl.BlockSpec((nsub, nb, two_half), lambda j, t: (j, 0, 0))],
        out_specs=[pl.BlockSpec((n_real, tc, cw), lambda j, t: (0, t, j)),
                   pl.BlockSpec((nsub, nb, two_half), lambda j, t: (j, 0, 0))],
        out_shape=[jax.ShapeDtypeStruct((n_real, t_len, w), F32),
                   jax.ShapeDtypeStruct((nj, nb, two_half), F32)],
        scratch_shapes=[pltpu.VMEM((nsub, tc * nb, LANES), F32),
                        pltpu.VMEM((nsub, tc * nb, two_half), F32),
                        pltpu.VMEM((nsub, tc * nb, LANES), F32),
                        pltpu.VMEM((nsub, nb, two_half), F32)],
        compiler_params=_params(("parallel", "arbitrary")),
    )(*([u] * len(u_specs)), bb, cc, abar, d.reshape(1, w), init)


def _s5_params(a_re, a_im, log_dt, b_re, b_im, c_re, c_im):
    g, p = a_re.shape
    ch = b_re.shape[2]
    gpb = LANES // ch
    nj = g // gpb
    dt = jnp.exp(log_dt)[:, None]
    mag = jnp.exp(dt * a_re)
    ang = dt * a_im
    abar_re = mag * jnp.cos(ang)
    abar_im = mag * jnp.sin(ang)
    den = a_re * a_re + a_im * a_im
    nr = abar_re - 1.0
    coef_re = (nr * a_re + abar_im * a_im) / den
    coef_im = (abar_im * a_re - nr * a_im) / den
    bb_re = coef_re[..., None] * b_re - coef_im[..., None] * b_im
    bb_im = coef_re[..., None] * b_im + coef_im[..., None] * b_re
    eye = jnp.eye(gpb, dtype=F32)

    def pack_in(m):
        m = m.reshape(nj, gpb, p, ch).transpose(0, 1, 3, 2)
        return jnp.einsum('jgcp,gh->jgchp', m, eye).reshape(nj, gpb * ch, gpb * p)

    def pack_out(m):
        m = m.reshape(nj, gpb, ch, p).transpose(0, 1, 3, 2)
        return jnp.einsum('jgpc,gh->jgphc', m, eye).reshape(nj, gpb * p, gpb * ch)

    bb = jnp.concatenate([pack_in(bb_re), pack_in(bb_im)], axis=2)
    cc = jnp.concatenate([pack_out(c_re), pack_out(-c_im)], axis=1)
    abar = jnp.concatenate([abar_re.reshape(nj, 1, gpb * p), abar_im.reshape(nj, 1, gpb * p)], axis=2)
    return bb, cc, abar


def _pack_state(s_re, s_im, nj):
    nb = s_re.shape[0]
    return jnp.concatenate([s_re.reshape(nb, nj, -1).transpose(1, 0, 2),
                            s_im.reshape(nb, nj, -1).transpose(1, 0, 2)], axis=2)


def _unpack_state(fin, n_real, g, p):
    nj, nb, two_half = fin.shape
    half = two_half // 2
    re = fin[:, :n_real, :half].transpose(1, 0, 2).reshape(n_real, g, p)
    im = fin[:, :n_real, half:].transpose(1, 0, 2).reshape(n_real, g, p)
    return re, im


def kernel(x_prompt, x_sample, cache_k, cache_v, cache_logf, state_ssm_re, state_ssm_im, page_table, p_prompt, p_sample, ffn1_norm, ffn1_w_gate, ffn1_w_up, ffn1_w_down, mix_norm, w_in, b_f, q_norm, k_norm, ssm_a_re, ssm_a_im, ssm_log_dt, ssm_b_re, ssm_b_im, ssm_c_re, ssm_c_im, ssm_d, ssm_w_glu, w_branch_a, w_branch_b, w_out, ffn2_norm, ffn2_w_gate, ffn2_w_up, ffn2_w_down, ple_norm, w_ple, w_ple_gate):
    depth = ffn1_norm.shape[0]
    assert depth == 1
    bsz, seq, dm = x_prompt.shape
    dbsz, dseq, _ = x_sample.shape
    n_heads = cache_k.shape[3]
    attn_w = n_heads * HEAD_DIM
    n_groups, ssm_p = ssm_a_re.shape[1], ssm_a_re.shape[2]
    ssm_w = n_groups * SSM_CH
    n_phys, page = cache_k.shape[1], cache_k.shape[2]
    mp = bsz * seq
    ms = dbsz * dseq
    o3 = 3 * attn_w
    o4 = o3 + n_heads

    pe_in = jnp.concatenate([p_prompt[0].reshape(mp, -1), p_sample[0].reshape(ms, -1)], axis=0).astype(BF16)

    x1, h2 = _ffn([x_prompt.reshape(mp, dm), x_sample.reshape(ms, dm)], ffn1_norm[0],
                  ffn1_w_gate[0], ffn1_w_up[0], ffn1_w_down[0], mix_norm[0])

    w_in_t = jnp.swapaxes(w_in[0], 0, 1)
    w_qkv_t = w_in_t[:o3].astype(BF16)
    w_rest_t = w_in_t[o4:].astype(BF16)
    w_f_t = jnp.pad(w_in_t[o3:o4], ((0, LANES - n_heads), (0, 0)))
    b_f_pad = jnp.pad(b_f[0], (0, LANES - n_heads)).reshape(1, LANES)
    qg = jnp.tile(q_norm[0], n_heads).reshape(1, attn_w)
    kg = jnp.tile(k_norm[0], n_heads).reshape(1, attn_w)
    (q,) = _mm([h2], [(w_qkv_t, 0, 0, True)], lambda a, r, c: [_head_rms(a[0], c[0])], [BF16], attn_w, vecs=[qg])
    kb_p, k_out_p = _kv_proj(h2, w_qkv_t, attn_w, kg, n_heads, normalise=True, row_off=0, m=mp)
    vb_p, v_out_p = _kv_proj(h2, w_qkv_t, 2 * attn_w, kg, n_heads, normalise=False, row_off=0, m=mp)
    kb_s, k_out_s = _kv_proj(h2, w_qkv_t, attn_w, kg, n_heads, normalise=True, row_off=mp, m=ms)
    vb_s, v_out_s = _kv_proj(h2, w_qkv_t, 2 * attn_w, kg, n_heads, normalise=False, row_off=mp, m=ms)
    (logf,) = _mm([h2], [(w_f_t, 0, 0, True)], lambda a, r, c: [_log_sigmoid(a[0] + c[0])], [F32], LANES,
                  vecs=[b_f_pad])
    (u,) = _mm([h2], [(w_rest_t, 0, 0, True)], lambda a, r, c: [a[0]], [F32], ssm_w)
    gate_a, gate_b = _mm([h2], [(w_rest_t, 0, ssm_w, True), (w_rest_t, 0, ssm_w + dm, True)],
                         lambda a, r, c: [jax.nn.sigmoid(a[0]), jax.nn.sigmoid(a[1])], [F32, F32], dm)

    lf_p = logf[:mp].reshape(bsz, seq, LANES)
    c_p = _cumsum_seq(lf_p)
    c_p_t = c_p[:, :, :n_heads].transpose(0, 2, 1)
    a_p = _attn_prompt(q, kb_p, vb_p, c_p, c_p_t, n_heads)

    n_pairs = n_heads // 2
    lf_s = logf[mp:].reshape(dbsz, dseq, LANES)
    lf_s_t = jnp.pad(lf_s[:, :, :n_heads].transpose(0, 2, 1), ((0, 0), (0, 0), (0, LANES - dseq)))
    pad_new = ((0, 0), (0, page - dseq), (0, 0))
    k_new = jnp.pad(kb_s.reshape(dbsz, dseq, attn_w), pad_new)
    v_new = jnp.pad(vb_s.reshape(dbsz, dseq, attn_w), pad_new)
    q2 = q[mp:].reshape(dbsz, dseq, 2, n_pairs, HEAD_DIM).transpose(0, 3, 2, 1, 4).reshape(dbsz, n_heads * dseq, HEAD_DIM)
    cache_lf2 = cache_logf[0].reshape(n_phys, page, 2, n_pairs).transpose(0, 3, 1, 2).reshape(n_phys, n_pairs, 2 * page)
    a_s2 = _attn_sample(q2, k_new, v_new, lf_s, lf_s_t,
                        cache_k[0].reshape(n_phys, page * n_heads, HEAD_DIM),
                        cache_v[0].reshape(n_phys, page * n_heads, HEAD_DIM),
                        cache_lf2, page_table, n_heads)
    a_s = a_s2.reshape(dbsz, n_pairs, 2, dseq, HEAD_DIM).transpose(0, 3, 2, 1, 4).reshape(ms, attn_w)
    a = jnp.concatenate([a_p, a_s], axis=0)

    bb, cc, abar = _s5_params(ssm_a_re[0], ssm_a_im[0], ssm_log_dt[0], ssm_b_re[0], ssm_b_im[0],
                              ssm_c_re[0], ssm_c_im[0])
    nj = bb.shape[0]
    d_vec = ssm_d[0].reshape(ssm_w)
    nb_p = -(-bsz // 8) * 8
    init_p = jnp.zeros((nj, nb_p, bb.shape[2]), F32)
    yg_p, fin_p = _s5(u, 0, bsz, seq, bb, cc, abar, d_vec, init_p, exact=False)
    nb_s = -(-dbsz // 8) * 8
    pad_state = ((0, nb_s - dbsz), (0, 0), (0, 0))
    init_s = _pack_state(jnp.pad(state_ssm_re[0], pad_state), jnp.pad(state_ssm_im[0], pad_state), nj)
    yg_s, fin_s = _s5(u, mp, dbsz, dseq, bb, cc, abar, d_vec, init_s, exact=True)
    yg = jnp.concatenate([yg_p.reshape(mp, ssm_w), yg_s.reshape(ms, ssm_w)], axis=0)
    (b_out,) = _mm([yg], [(ssm_w_glu[0], 0, 0, False)], lambda a, r, c: [r[0] * jax.nn.sigmoid(a[0])], [BF16],
                   ssm_w, rows=[(yg, 0)], tm_target=528)

    (merged,) = _mm([a, b_out], [(w_branch_a[0], 0, 0, False), (w_branch_b[0], 1, 0, False)],
                    lambda acc, r, c: [r[0] * acc[0] + r[1] * acc[1]], [BF16], dm,
                    rows=[(gate_a, 0), (gate_b, 0)])
    (x2,) = _mm([merged], [(w_out[0], 0, 0, False)], lambda acc, r, c: [r[0] + acc[0]], [F32], dm, rows=[(x1, 0)])

    x3, h4 = _ffn([x2], ffn2_norm[0], ffn2_w_gate[0], ffn2_w_up[0], ffn2_w_down[0], ple_norm[0])

    def ple(row_off, rows):
        (y,) = _mm([pe_in, h4], [(w_ple[0], 0, 0, False), (w_ple_gate[0], 1, 0, False)],
                   lambda acc, r, c: [r[0] + acc[0] * jax.nn.sigmoid(acc[1])], [F32], dm, rows=[(x3, 0)],
                   row_off=row_off, m=rows)
        return y

    y_p = ple(0, mp)
    y_s = ple(mp, ms)

    sre_p, sim_p = _unpack_state(fin_p, bsz, n_groups, ssm_p)
    sre_s, sim_s = _unpack_state(fin_s, dbsz, n_groups, ssm_p)
    return (y_p.reshape(bsz, seq, dm), y_s.reshape(dbsz, dseq, dm),
            k_out_p.reshape(1, bsz, seq, n_heads, HEAD_DIM), v_out_p.reshape(1, bsz, seq, n_heads, HEAD_DIM),
            lf_p[:, :, :n_heads].reshape(1, bsz, seq, n_heads),
            sre_p[None], sim_p[None],
            k_out_s.reshape(1, dbsz, dseq, n_heads, HEAD_DIM), v_out_s.reshape(1, dbsz, dseq, n_heads, HEAD_DIM),
            lf_s[:, :, :n_heads].reshape(1, dbsz, dseq, n_heads),
            sre_s[None], sim_s[None])
```
